```python
import jax, jax.numpy as jnp
from jax import lax
import numpy as np

D_MODEL = 4096
BATCH = 2
SEQ = 4096
DEPTH = 1
DEC_BATCH = 32
DEC_SEQ = 4
PAST_LEN = 8192
PAGE_SIZE = 128

HEAD_DIM = 128
MIX_WIDTH = D_MODEL
ATT_WIDTH = MIX_WIDTH // 2
REC_WIDTH = MIX_WIDTH - ATT_WIDTH
N_ATT_HEADS = ATT_WIDTH // HEAD_DIM
N_REC_HEADS = REC_WIDTH // HEAD_DIM
DILATED_PATTERNS = ((128, 1), (512, 4), (2048, 16))
WIN_BUF = 2048
LOCAL_BLOCK = 128
ATT_SCALE = HEAD_DIM ** -0.5
ROT_DIM = HEAD_DIM // 4
ROPE_THETA = 500000.0
HGRN_CHUNK = 32
N_MEM = 256
MEM_HEADS = 4
MEM_HEAD_DIM = 128
MEM_WIDTH = MEM_HEADS * MEM_HEAD_DIM
PEER_N_KEYS = 128
PEER_N_EXPERTS = PEER_N_KEYS * PEER_N_KEYS
PEER_HEADS = 8
PEER_KEY_DIM = 256
PEER_HALF = PEER_KEY_DIM // 2
PEER_TOPK = 16
PEER_BLOCK = 64
EPS = 1e-6
NEG_INF = -1e30
IN_SPLITS = (ATT_WIDTH, 2 * ATT_WIDTH, 3 * ATT_WIDTH, 3 * ATT_WIDTH + REC_WIDTH, 3 * ATT_WIDTH + 2 * REC_WIDTH, 3 * ATT_WIDTH + 3 * REC_WIDTH)
IN_COLS = 3 * ATT_WIDTH + 4 * REC_WIDTH
F32 = jnp.float32

kernel_name = 'hybrid_dilated_hgrn2_peer_step'


def rmsnorm(x, g):
    xf = x.astype(F32)
    y = xf * lax.rsqrt(jnp.mean(xf * xf, axis=-1, keepdims=True) + EPS)
    return (y * g.astype(F32)).astype(x.dtype)


def partial_rope(x, pos):
    half = ROT_DIM // 2
    inv = ROPE_THETA ** (-jnp.arange(half, dtype=F32) / half)
    ang = pos.astype(F32)[:, None] * inv[None, :]
    cos, sin = jnp.cos(ang)[:, None, :], jnp.sin(ang)[:, None, :]
    xr = x[..., :ROT_DIM].astype(F32)
    x1, x2 = xr[..., :half], xr[..., half:]
    rot = jnp.concatenate([x1 * cos - x2 * sin, x2 * cos + x1 * sin], axis=-1)
    return jnp.concatenate([rot.astype(x.dtype), x[..., ROT_DIM:]], axis=-1)


def mixer_inputs(xn, w_in, lb, pos):
    B, T, _ = xn.shape
    z = xn @ w_in
    qa, ka, va, qr, fr, ir, gr = jnp.split(z, IN_SPLITS, axis=-1)
    heads = lambda t, h: t.reshape(B, T, h, HEAD_DIM)
    qa = partial_rope(heads(qa, N_ATT_HEADS), pos)
    ka = partial_rope(heads(ka, N_ATT_HEADS), pos)
    va = heads(va, N_ATT_HEADS)
    fz = heads(fr, N_REC_HEADS).astype(F32)
    lbh = lb.reshape(N_REC_HEADS, HEAD_DIM)
    logf = jnp.log(lbh + (1.0 - lbh) * jax.nn.sigmoid(fz))
    kr = (1.0 - lbh) * jax.nn.sigmoid(-fz)
    qr = jax.nn.silu(heads(qr, N_REC_HEADS).astype(F32))
    vr = heads(ir, N_REC_HEADS).astype(F32)
    return qa, ka, va, qr, kr, vr, logf, gr


def banded_stats(q, k, v, n_back):
    N, L, H, Dh = q.shape
    blk = LOCAL_BLOCK
    Lp = -(-L // blk) * blk
    nb = Lp // blk
    qb = jnp.pad(q.astype(F32), ((0, 0), (0, Lp - L), (0, 0), (0, 0))).reshape(N, nb, blk, H, Dh)
    def bands(x):
        xp = jnp.pad(x.astype(F32), ((0, 0), (blk, Lp - L), (0, 0), (0, 0)))
        return jnp.concatenate([xp[:, :Lp].reshape(N, nb, blk, H, Dh), xp[:, blk:].reshape(N, nb, blk, H, Dh)], axis=2)
    kb, vb = bands(k), bands(v)
    sc = jnp.einsum('nbqhd,nbkhd->nbhqk', qb, kb) * ATT_SCALE
    iq = jnp.arange(blk)[:, None]
    ik = jnp.arange(2 * blk)[None, :]
    off = iq + blk - ik
    kpos = jnp.arange(nb)[:, None, None] * blk - blk + ik[None]
    mask = ((off >= 0) & (off <= n_back))[None] & (kpos >= 0)
    sc = jnp.where(mask[None, :, None], sc, NEG_INF)
    m = jnp.max(sc, axis=-1)
    p = jnp.exp(sc - m[..., None])
    s = jnp.sum(p, axis=-1)
    o = jnp.einsum('nbhqk,nbkhd->nbqhd', p, vb).reshape(N, Lp, H, Dh)[:, :L]
    m = jnp.swapaxes(m, 2, 3).reshape(N, Lp, H)[:, :L]
    s = jnp.swapaxes(s, 2, 3).reshape(N, Lp, H)[:, :L]
    return m, s, o


def combine_by_denominators(stats):
    big = stats[0][0]
    for m, _, _ in stats[1:]:
        big = jnp.maximum(big, m)
    num, den = 0.0, 0.0
    for m, s, o in stats:
        w = jnp.exp(m - big)
        num = num + w[..., None] * o
        den = den + w * s
    return num / den[..., None]


def dilated_attention_prompt(q, k, v):
    B, S, H, Dh = q.shape
    stats = []
    for window, dil in DILATED_PATTERNS:
        L = S // dil
        sub = lambda x: jnp.swapaxes(x.reshape(B, L, dil, H, Dh), 1, 2).reshape(B * dil, L, H, Dh)
        m, s, o = banded_stats(sub(q), sub(k), sub(v), window // dil)
        back = lambda x: jnp.swapaxes(x.reshape((B, dil, L) + x.shape[2:]), 1, 2).reshape((B, S) + x.shape[2:])
        stats.append((back(m), back(s), back(o)))
    return combine_by_denominators(stats)


def dilated_attention_sample(q, k_all, v_all):
    T = q.shape[1]
    j = jnp.arange(T)
    qf = q.astype(F32)
    stats = []
    for window, dil in DILATED_PATTERNS:
        taps = jnp.arange(window // dil + 1)
        rel = j[:, None] - taps[None, :] * dil
        idx = WIN_BUF + rel
        valid = (PAST_LEN + rel) >= 0
        kg = k_all[:, idx].astype(F32)
        vg = v_all[:, idx].astype(F32)
        sc = jnp.einsum('bthd,btkhd->bthk', qf, kg) * ATT_SCALE
        sc = jnp.where(valid[None, :, None, :], sc, NEG_INF)
        m = jnp.max(sc, axis=-1)
        p = jnp.exp(sc - m[..., None])
        stats.append((m, jnp.sum(p, axis=-1), jnp.einsum('bthk,btkhd->bthd', p, vg)))
    return combine_by_denominators(stats)


def hgrn2_chunked(q, k, v, logf, s0):
    B, L, H, Dk = q.shape
    Dv = v.shape[-1]
    C = min(HGRN_CHUNK, L)
    Lp = -(-L // C) * C
    nc = Lp // C
    def chunks(x):
        x = jnp.pad(x, ((0, 0), (0, Lp - L), (0, 0), (0, 0)))
        return x.reshape(B, nc, C, H, x.shape[-1]).transpose(1, 0, 3, 2, 4)
    causal = jnp.tril(jnp.ones((C, C), dtype=bool))
    def step(S, blk):
        qc, kc, vc, gc = blk
        b = jnp.cumsum(gc, axis=2)
        b_last = b[:, :, -1:, :]
        q_dec = qc * jnp.exp(b)
        att = jnp.where(causal, jnp.einsum('bhtk,bhsk->bhts', q_dec, kc * jnp.exp(-b)), 0.0)
        o = jnp.einsum('bhts,bhsv->bhtv', att, vc) + jnp.einsum('bhtk,bhkv->bhtv', q_dec, S)
        S = jnp.exp(b_last[:, :, 0, :])[..., None] * S + jnp.einsum('bhsk,bhsv->bhkv', kc * jnp.exp(b_last - b), vc)
        return S, o
    s_fin, o = lax.scan(step, s0.astype(F32), (chunks(q), chunks(k), chunks(v), chunks(logf)))
    o = o.transpose(1, 0, 3, 2, 4).reshape(B, Lp, H, Dv)[:, :L]
    return o, s_fin


def merge_groups(o_att, o_rec, gr, g_att, g_rec, w_out):
    B, T = o_att.shape[:2]
    a = rmsnorm(o_att, g_att.reshape(N_ATT_HEADS, HEAD_DIM)).reshape(B, T, ATT_WIDTH)
    r = rmsnorm(o_rec, g_rec.reshape(N_REC_HEADS, HEAD_DIM)).reshape(B, T, REC_WIDTH) * jax.nn.silu(gr.astype(F32))
    return jnp.concatenate([a, r], axis=-1).astype(w_out.dtype) @ w_out


def window_tail(x):
    front = max(0, WIN_BUF - x.shape[1])
    return jnp.pad(x, ((0, 0), (front, 0), (0, 0), (0, 0)))[:, -WIN_BUF:]


def memory_kv(mem, g, w_k, w_v):
    B, M, _ = mem.shape
    mn = rmsnorm(mem, g)
    return (mn @ w_k).reshape(B, M, MEM_HEADS, MEM_HEAD_DIM), (mn @ w_v).reshape(B, M, MEM_HEADS, MEM_HEAD_DIM)


def memory_attend(xn, w_q, mk, mv, w_o):
    B, T, _ = xn.shape
    q = (xn @ w_q).reshape(B, T, MEM_HEADS, MEM_HEAD_DIM).astype(F32)
    sc = jnp.einsum('bthd,bmhd->bhtm', q, mk.astype(F32)) * (MEM_HEAD_DIM ** -0.5)
    p = jax.nn.softmax(sc, axis=-1)
    o = jnp.einsum('bhtm,bmhd->bthd', p, mv.astype(F32)).reshape(B, T, MEM_WIDTH)
    return o.astype(xn.dtype) @ w_o


def peer_ffn(xn, w_query, sub_keys, expert_u, expert_v):
    B, T, D = xn.shape
    n_tok = B * T
    xt = xn.reshape(n_tok, D)
    qry = (xt @ w_query).astype(F32).reshape(n_tok, PEER_HEADS, 2, PEER_HALF)
    sc = jnp.einsum('thpc,hpnc->thpn', qry, sub_keys.astype(F32))
    s1, i1 = lax.top_k(sc[:, :, 0], PEER_TOPK)
    s2, i2 = lax.top_k(sc[:, :, 1], PEER_TOPK)
    cand_s = (s1[..., :, None] + s2[..., None, :]).reshape(n_tok, PEER_HEADS, PEER_TOPK * PEER_TOPK)
    cand_i = (i1[..., :, None] * PEER_N_KEYS + i2[..., None, :]).reshape(n_tok, PEER_HEADS, PEER_TOPK * PEER_TOPK)
    top_s, top_pos = lax.top_k(cand_s, PEER_TOPK)
    eidx = jnp.take_along_axis(cand_i, top_pos, axis=-1)
    gate = jax.nn.softmax(top_s, axis=-1)
    pad = (-n_tok) % PEER_BLOCK
    nb = (n_tok + pad) // PEER_BLOCK
    xb = jnp.pad(xt, ((0, pad), (0, 0))).reshape(nb, PEER_BLOCK, D)
    ib = jnp.pad(eidx, ((0, pad), (0, 0), (0, 0))).reshape(nb, PEER_BLOCK, PEER_HEADS, PEER_TOPK)
    gb = jnp.pad(gate, ((0, pad), (0, 0), (0, 0))).reshape(nb, PEER_BLOCK, PEER_HEADS, PEER_TOPK)
    def block(args):
        xk, ik, gk = args
        h = jnp.einsum('td,thkd->thk', xk, expert_u[ik]).astype(F32)
        wgt = (gk * jax.nn.gelu(h, approximate=False)).astype(xk.dtype)
        return jnp.einsum('thk,thkd->td', wgt, expert_v[ik])
    y = lax.map(block, (xb, ib, gb))
    return y.reshape(nb * PEER_BLOCK, D)[:n_tok].reshape(B, T, D)


def setup_inputs(seed: int = 0) -> dict:
    key = jax.random.key(seed)
    ks = jax.random.split(key, 28)
    nrm = lambda k, shape, scale: jax.random.normal(k, shape, F32) * scale
    gain = lambda k, shape: 1.0 + 0.02 * jax.random.normal(k, shape, F32)
    return {
        'x_prompt': nrm(ks[0], (BATCH, SEQ, D_MODEL), 1.0),
        'x_sample': nrm(ks[1], (DEC_BATCH, DEC_SEQ, D_MODEL), 1.0),
        'cache_win_k': nrm(ks[2], (DEPTH, DEC_BATCH, WIN_BUF, N_ATT_HEADS, HEAD_DIM), 1.0),
        'cache_win_v': nrm(ks[3], (DEPTH, DEC_BATCH, WIN_BUF, N_ATT_HEADS, HEAD_DIM), 1.0),
        'state_hgrn': nrm(ks[4], (DEPTH, DEC_BATCH, N_REC_HEADS, HEAD_DIM, HEAD_DIM), 0.5),
        'cache_mem_k': nrm(ks[5], (DEPTH, DEC_BATCH, N_MEM, MEM_HEADS, MEM_HEAD_DIM), 1.0),
        'cache_mem_v': nrm(ks[6], (DEPTH, DEC_BATCH, N_MEM, MEM_HEADS, MEM_HEAD_DIM), 1.0),
        'mem_prompt': nrm(ks[7], (BATCH, N_MEM, D_MODEL), 1.0),
        'norm_mix': gain(ks[8], (DEPTH, D_MODEL)),
        'w_in': nrm(ks[9], (DEPTH, D_MODEL, IN_COLS), D_MODEL ** -0.5),
        'lb_table': nrm(ks[10], (DEPTH + 1, REC_WIDTH), 0.1),
        'g_att_out': gain(ks[11], (DEPTH, ATT_WIDTH)),
        'g_rec_out': gain(ks[12], (DEPTH, REC_WIDTH)),
        'w_out': nrm(ks[13], (DEPTH, MIX_WIDTH, D_MODEL), MIX_WIDTH ** -0.5),
        'norm_mem_x': gain(ks[14], (DEPTH, D_MODEL)),
        'norm_mem_kv': gain(ks[15], (DEPTH, D_MODEL)),
        'w_mem_q': nrm(ks[16], (DEPTH, D_MODEL, MEM_WIDTH), D_MODEL ** -0.5),
        'w_mem_k': nrm(ks[17], (DEPTH, D_MODEL, MEM_WIDTH), D_MODEL ** -0.5),
        'w_mem_v': nrm(ks[18], (DEPTH, D_MODEL, MEM_WIDTH), D_MODEL ** -0.5),
        'w_mem_o': nrm(ks[19], (DEPTH, MEM_WIDTH, D_MODEL), MEM_WIDTH ** -0.5),
        'norm_ffn': gain(ks[20], (DEPTH, D_MODEL)),
        'peer_w_query': nrm(ks[21], (DEPTH, D_MODEL, PEER_HEADS * PEER_KEY_DIM), D_MODEL ** -0.5),
        'peer_sub_keys': nrm(ks[22], (DEPTH, PEER_HEADS, 2, PEER_N_KEYS, PEER_HALF), PEER_HALF ** -0.5),
        'peer_u': nrm(ks[23], (DEPTH, PEER_N_EXPERTS, D_MODEL), D_MODEL ** -0.5),
        'peer_v': nrm(ks[24], (DEPTH, PEER_N_EXPERTS, D_MODEL), 0.5),
        'norm_final': gain(ks[25], (D_MODEL,)),
    }


def reference(x_prompt, x_sample, cache_win_k, cache_win_v, state_hgrn, cache_mem_k, cache_mem_v, mem_prompt,
              norm_mix, w_in, lb_table, g_att_out, g_rec_out, w_out, norm_mem_x, norm_mem_kv,
              w_mem_q, w_mem_k, w_mem_v, w_mem_o, norm_ffn, peer_w_query, peer_sub_keys, peer_u, peer_v,
              norm_final):
    pos_p = jnp.arange(SEQ, dtype=jnp.int32)
    pos_s = PAST_LEN + jnp.arange(DEC_SEQ, dtype=jnp.int32)
    lower_bounds = jnp.cumsum(jax.nn.softmax(lb_table.astype(F32), axis=0), axis=0)
    hp, hs = x_prompt, x_sample
    wk_p, wv_p, st_p, mk_p, mv_p, wk_s, wv_s, st_s = [], [], [], [], [], [], [], []
    for l in range(DEPTH):
        lb = lower_bounds[l]
        qa, ka, va, qr, kr, vr, logf, gr = mixer_inputs(rmsnorm(hp, norm_mix[l]), w_in[l], lb, pos_p)
        o_att = dilated_attention_prompt(qa, ka, va)
        s_zero = jnp.zeros((hp.shape[0], N_REC_HEADS, HEAD_DIM, HEAD_DIM), F32)
        o_rec, s_fin = hgrn2_chunked(qr, kr, vr, logf, s_zero)
        hp = hp + merge_groups(o_att, o_rec, gr, g_att_out[l], g_rec_out[l], w_out[l])
        wk_p.append(window_tail(ka))
        wv_p.append(window_tail(va))
        st_p.append(s_fin.astype(hp.dtype))
        qa, ka, va, qr, kr, vr, logf, gr = mixer_inputs(rmsnorm(hs, norm_mix[l]), w_in[l], lb, pos_s)
        k_all = jnp.concatenate([cache_win_k[l].astype(ka.dtype), ka], axis=1)
        v_all = jnp.concatenate([cache_win_v[l].astype(va.dtype), va], axis=1)
        o_att = dilated_attention_sample(qa, k_all, v_all)
        o_rec, s_new = hgrn2_chunked(qr, kr, vr, logf, state_hgrn[l])
        hs = hs + merge_groups(o_att, o_rec, gr, g_att_out[l], g_rec_out[l], w_out[l])
        wk_s.append(k_all[:, -WIN_BUF:])
        wv_s.append(v_all[:, -WIN_BUF:])
        st_s.append(s_new.astype(hs.dtype))
        mk, mv = memory_kv(mem_prompt, norm_mem_kv[l], w_mem_k[l], w_mem_v[l])
        hp = hp + memory_attend(rmsnorm(hp, norm_mem_x[l]), w_mem_q[l], mk, mv, w_mem_o[l])
        hs = hs + memory_attend(rmsnorm(hs, norm_mem_x[l]), w_mem_q[l], cache_mem_k[l], cache_mem_v[l], w_mem_o[l])
        mk_p.append(mk)
        mv_p.append(mv)
        hp = hp + peer_ffn(rmsnorm(hp, norm_ffn[l]), peer_w_query[l], peer_sub_keys[l], peer_u[l], peer_v[l])
        hs = hs + peer_ffn(rmsnorm(hs, norm_ffn[l]), peer_w_query[l], peer_sub_keys[l], peer_u[l], peer_v[l])
    y_prompt = rmsnorm(hp, norm_final)
    y_sample = rmsnorm(hs, norm_final)
    return (y_prompt, y_sample, jnp.stack(wk_p), jnp.stack(wv_p), jnp.stack(st_p), jnp.stack(mk_p), jnp.stack(mv_p), jnp.stack(wk_s), jnp.stack(wv_s), jnp.stack(st_s))
```

```python
import functools

import jax
import jax.numpy as jnp
from jax import lax
from jax.experimental import pallas as pl
from jax.experimental.pallas import tpu as pltpu

F32 = jnp.float32
BF16 = jnp.bfloat16

D_MODEL = 4096
HEAD_DIM = 128
ATT_WIDTH = 2048
REC_WIDTH = 2048
N_ATT_HEADS = 16
N_REC_HEADS = 16
IN_COLS = 3 * ATT_WIDTH + 4 * REC_WIDTH
DILATED_PATTERNS = ((128, 1), (512, 4), (2048, 16))
WIN_BUF = 2048
PAST_LEN = 8192
ATT_SCALE = HEAD_DIM ** -0.5
ROT_HALF = HEAD_DIM // 8
ROPE_THETA = 500000.0
N_MEM = 256
MEM_HEADS = 4
MEM_WIDTH = 512
PEER_N_KEYS = 128
PEER_N_EXPERTS = PEER_N_KEYS * PEER_N_KEYS
PEER_HEADS = 8
PEER_TOPK = 16
EPS = 1e-6
NEG = -1e30

LANES = 128
VMEM_LIMIT = 56 * 1024 * 1024

NT_DIMS = (((1,), (1,)), ((), ()))
TN_DIMS = (((0,), (0,)), ((), ()))


def _params(*sem):
    return pltpu.CompilerParams(dimension_semantics=sem, vmem_limit_bytes=VMEM_LIMIT)


def _rms(x, g):
    return x * lax.rsqrt(jnp.mean(x * x, axis=-1, keepdims=True) + EPS) * g


def _norm_matmul_kernel(*refs, rope_tiles, heads_per_tile):
    if rope_tiles:
        x_ref, g_ref, w_ref, cos_ref, sa_ref, sb_ref, o_ref, xn_ref = refs
    else:
        x_ref, g_ref, w_ref, o_ref, xn_ref = refs
    j = pl.program_id(1)

    @pl.when(j == 0)
    def _():
        xn_ref[...] = _rms(x_ref[...], g_ref[...]).astype(BF16)

    z = jnp.dot(xn_ref[...], w_ref[...], preferred_element_type=F32)
    if rope_tiles:
        @pl.when(j < rope_tiles)
        def _():
            cos, sa, sb = cos_ref[...], sa_ref[...], sb_ref[...]
            for h in range(heads_per_tile):
                zh = z[:, h * LANES:(h + 1) * LANES]
                o_ref[:, h * LANES:(h + 1) * LANES] = (
                    zh * cos + pltpu.roll(zh, LANES - ROT_HALF, 1) * sa + pltpu.roll(zh, ROT_HALF, 1) * sb)

        @pl.when(j >= rope_tiles)
        def _():
            o_ref[...] = z
    else:
        o_ref[...] = z.astype(o_ref.dtype)


def norm_matmul(x, g, w, *, rope=None, rope_cols=0, out_dtype=F32, tm=512, tn=512):
    m, k = x.shape
    n = w.shape[1]
    tm, tn = min(tm, m), min(tn, n)
    assert m % tm == 0 and n % tn == 0 and rope_cols % tn == 0
    in_specs = [pl.BlockSpec((tm, k), lambda i, j: (i, 0)),
                pl.BlockSpec((1, k), lambda i, j: (0, 0)),
                pl.BlockSpec((k, tn), lambda i, j: (0, j))]
    args = [x, g.reshape(1, k), w]
    rope_tiles = 0
    if rope is not None:
        cos, sa, sb = rope
        period = cos.shape[0] // tm
        rope_tiles = rope_cols // tn
        spec = pl.BlockSpec((tm, LANES), lambda i, j: (i % period, 0))
        in_specs += [spec, spec, spec]
        args += [cos, sa, sb]
    return pl.pallas_call(
        functools.partial(_norm_matmul_kernel, rope_tiles=rope_tiles, heads_per_tile=tn // LANES),
        grid=(m // tm, n // tn),
        in_specs=in_specs,
        out_specs=pl.BlockSpec((tm, tn), lambda i, j: (i, j)),
        out_shape=jax.ShapeDtypeStruct((m, n), out_dtype),
        scratch_shapes=[pltpu.VMEM((tm, k), BF16)],
        compiler_params=_params("parallel", "arbitrary"),
        name="norm_matmul",
    )(*args)


def _rope_tables(pos):
    inv = ROPE_THETA ** (-jnp.arange(ROT_HALF, dtype=F32) / ROT_HALF)
    ang = pos.astype(F32)[:, None] * inv[None, :]
    cos, sin = jnp.cos(ang), jnp.sin(ang)
    t = pos.shape[0]
    rest = LANES - 2 * ROT_HALF
    cos_t = jnp.concatenate([cos, cos, jnp.ones((t, rest), F32)], axis=1)
    sa = jnp.concatenate([-sin, jnp.zeros((t, LANES - ROT_HALF), F32)], axis=1)
    sb = jnp.concatenate([jnp.zeros((t, ROT_HALF), F32), sin, jnp.zeros((t, rest), F32)], axis=1)
    return cos_t, sa, sb


def _matmul_residual_kernel(a_ref, w_ref, r_ref, o_ref):
    o_ref[...] = r_ref[...] + jnp.dot(a_ref[...], w_ref[...], preferred_element_type=F32)


def matmul_residual(a, w, res, *, tm=512, tn=512):
    m, k = a.shape
    n = w.shape[1]
    tm, tn = min(tm, m), min(tn, n)
    assert m % tm == 0 and n % tn == 0
    return pl.pallas_call(
        _matmul_residual_kernel,
        grid=(m // tm, n // tn),
        in_specs=[pl.BlockSpec((tm, k), lambda i, j: (i, 0)),
                  pl.BlockSpec((k, tn), lambda i, j: (0, j)),
                  pl.BlockSpec((tm, tn), lambda i, j: (i, j))],
        out_specs=pl.BlockSpec((tm, tn), lambda i, j: (i, j)),
        out_shape=jax.ShapeDtypeStruct((m, n), F32),
        compiler_params=_params("parallel", "parallel"),
        name="matmul_residual",
    )(a, w, res)


ATT_TQ = 256
ATT_BACK = WIN_BUF // ATT_TQ


def _distance_multiplicity(delta):
    c = jnp.zeros(delta.shape, F32)
    for window, dil in DILATED_PATTERNS:
        c = c + ((delta >= 0) & (delta <= window) & (delta % dil == 0)).astype(F32)
    return c


def _log_multiplicity(delta):
    c = _distance_multiplicity(delta)
    return jnp.where(c > 0, jnp.log(jnp.maximum(c, 1.0)), NEG)


def _attn_prompt_kernel(q_ref, k_ref, v_ref, g_ref, bias_ref, o_ref, kb_ref, vb_ref):
    i = pl.program_id(2)

    @pl.when(i == 0)
    def _():
        kb_ref[...] = k_ref[...].astype(BF16)
        vb_ref[...] = v_ref[...].astype(BF16)

    q = (q_ref[...] * ATT_SCALE).astype(BF16)

    def body(kb, carry):
        m, l, acc = carry
        start = pl.multiple_of(kb * ATT_TQ, ATT_TQ)
        s = lax.dot_general(q, kb_ref[pl.ds(start, ATT_TQ), :], NT_DIMS, preferred_element_type=F32)
        s = s + bias_ref[i - kb]
        m_new = jnp.maximum(m, jnp.max(s, axis=-1, keepdims=True))
        alpha = jnp.exp(m - m_new)
        p = jnp.exp(s - m_new)
        l = alpha * l + jnp.sum(p, axis=-1, keepdims=True)
        acc = alpha * acc + jnp.dot(p.astype(BF16), vb_ref[pl.ds(start, ATT_TQ), :], preferred_element_type=F32)
        return m_new, l, acc

    init = (jnp.full((ATT_TQ, 1), NEG, F32), jnp.zeros((ATT_TQ, 1), F32), jnp.zeros((ATT_TQ, HEAD_DIM), F32))
    _, l, acc = lax.fori_loop(jnp.maximum(i - ATT_BACK, 0), i + 1, body, init)
    o_ref[...] = _rms(acc / l, g_ref[...]).astype(o_ref.dtype)


def attention_prompt(z, g_att):
    b, s, _ = z.shape
    rq = jnp.arange(ATT_TQ)[None, :, None]
    rk = jnp.arange(ATT_TQ)[None, None, :]
    off = jnp.arange(ATT_BACK + 1)[:, None, None]
    bias = _log_multiplicity(off * ATT_TQ + rq - rk)
    kcol, vcol = ATT_WIDTH // HEAD_DIM, 2 * ATT_WIDTH // HEAD_DIM
    return pl.pallas_call(
        _attn_prompt_kernel,
        grid=(b, N_ATT_HEADS, s // ATT_TQ),
        in_specs=[pl.BlockSpec((None, ATT_TQ, HEAD_DIM), lambda bi, h, i: (bi, i, h)),
                  pl.BlockSpec((None, s, HEAD_DIM), lambda bi, h, i: (bi, 0, kcol + h)),
                  pl.BlockSpec((None, s, HEAD_DIM), lambda bi, h, i: (bi, 0, vcol + h)),
                  pl.BlockSpec((1, HEAD_DIM), lambda bi, h, i: (0, h)),
                  pl.BlockSpec((ATT_BACK + 1, ATT_TQ, ATT_TQ), lambda bi, h, i: (0, 0, 0))],
        out_specs=pl.BlockSpec((None, ATT_TQ, HEAD_DIM), lambda bi, h, i: (bi, i, h)),
        out_shape=jax.ShapeDtypeStruct((b, s, ATT_WIDTH), BF16),
        scratch_shapes=[pltpu.VMEM((s, HEAD_DIM), BF16), pltpu.VMEM((s, HEAD_DIM), BF16)],
        compiler_params=_params("parallel", "parallel", "arbitrary"),
        name="attention_prompt",
    )(z, z, z, g_att.reshape(1, ATT_WIDTH), bias)


SMP_TB = 256
SMP_ROWS = SMP_TB * N_ATT_HEADS


def _attn_sample_kernel(q_ref, kc_ref, vc_ref, kx_ref, vx_ref, kn_ref, vn_ref, bias_ref, biasn_ref, g_ref,
                        wk_ref, wv_ref, o_ref, m_ref, l_ref, acc_ref, *, t_new):
    kblk = pl.program_id(1)
    last = pl.num_programs(1) - 1

    @pl.when(kblk == 0)
    def _():
        m_ref[...] = jnp.full(m_ref.shape, NEG, F32)
        l_ref[...] = jnp.zeros(l_ref.shape, F32)
        acc_ref[...] = jnp.zeros(acc_ref.shape, F32)

    wk_ref[0:SMP_TB - t_new] = kc_ref[t_new:SMP_TB]
    wv_ref[0:SMP_TB - t_new] = vc_ref[t_new:SMP_TB]

    @pl.when(kblk < last)
    def _():
        wk_ref[SMP_TB - t_new:SMP_TB] = kx_ref[...]
        wv_ref[SMP_TB - t_new:SMP_TB] = vx_ref[...]

    @pl.when(kblk == last)
    def _():
        wk_ref[SMP_TB - t_new:SMP_TB] = kn_ref[...]
        wv_ref[SMP_TB - t_new:SMP_TB] = vn_ref[...]

    q = q_ref[...]

    def update(k2, v2, bias):
        s = lax.dot_general(q, k2.astype(BF16), NT_DIMS, preferred_element_type=F32) + bias
        m_old = m_ref[...]
        m_new = jnp.maximum(m_old, jnp.max(s, axis=-1, keepdims=True))
        alpha = jnp.exp(m_old - m_new)
        p = jnp.exp(s - m_new)
        l_ref[...] = alpha * l_ref[...] + jnp.sum(p, axis=-1, keepdims=True)
        acc_ref[...] = alpha * acc_ref[...] + jnp.dot(p.astype(BF16), v2.astype(BF16), preferred_element_type=F32)
        m_ref[...] = m_new

    update(kc_ref[...].reshape(SMP_ROWS, HEAD_DIM), vc_ref[...].reshape(SMP_ROWS, HEAD_DIM), bias_ref[...])

    @pl.when(kblk == last)
    def _():
        update(kn_ref[...].reshape(t_new * N_ATT_HEADS, HEAD_DIM), vn_ref[...].reshape(t_new * N_ATT_HEADS, HEAD_DIM),
               biasn_ref[...])
        o_ref[...] = _rms(acc_ref[...] / l_ref[...], g_ref[...])


def attention_sample(q, k_new, v_new, cache_k, cache_v, g_att):
    bd, t_new, nh, dh = q.shape
    nblk = WIN_BUF // SMP_TB
    qt = (q * ATT_SCALE).transpose(0, 2, 1, 3).reshape(bd, nh * t_new, dh).astype(BF16)
    hq = jnp.arange(nh * t_new)[:, None] // t_new
    jq = jnp.arange(nh * t_new)[:, None] % t_new
    def make_bias(tok, hk):
        return jnp.where(hq == hk, _log_multiplicity(WIN_BUF + jq - tok), NEG)
    cols = jnp.arange(WIN_BUF * nh)[None, :]
    bias = make_bias(cols // nh, cols % nh).reshape(nh * t_new, nblk, SMP_ROWS).transpose(1, 0, 2)
    cols_n = jnp.arange(t_new * nh)[None, :]
    bias_n = make_bias(WIN_BUF + cols_n // nh, cols_n % nh)
    g_rows = jnp.repeat(g_att.reshape(nh, dh), t_new, axis=0)
    sub = SMP_TB // t_new
    cache_blk = pl.BlockSpec((None, SMP_TB, nh, dh), lambda b, k: (b, k, 0, 0))
    next_blk = pl.BlockSpec((None, t_new, nh, dh), lambda b, k: (b, jnp.minimum((k + 1) * sub, WIN_BUF // t_new - 1), 0, 0))
    new_blk = pl.BlockSpec((None, t_new, nh, dh), lambda b, k: (b, 0, 0, 0))
    wk, wv, o = pl.pallas_call(
        functools.partial(_attn_sample_kernel, t_new=t_new),
        grid=(bd, nblk),
        in_specs=[pl.BlockSpec((None, nh * t_new, dh), lambda b, k: (b, 0, 0)),
                  cache_blk, cache_blk, next_blk, next_blk, new_blk, new_blk,
                  pl.BlockSpec((None, nh * t_new, SMP_ROWS), lambda b, k: (k, 0, 0)),
                  pl.BlockSpec((nh * t_new, nh * t_new), lambda b, k: (0, 0)),
                  pl.BlockSpec((nh * t_new, dh), lambda b, k: (0, 0))],
        out_specs=[cache_blk, cache_blk, pl.BlockSpec((None, nh * t_new, dh), lambda b, k: (b, 0, 0))],
        out_shape=[jax.ShapeDtypeStruct(cache_k.shape, F32), jax.ShapeDtypeStruct(cache_v.shape, F32),
                   jax.ShapeDtypeStruct((bd, nh * t_new, dh), F32)],
        scratch_shapes=[pltpu.VMEM((nh * t_new, 1), F32), pltpu.VMEM((nh * t_new, 1), F32),
                        pltpu.VMEM((nh * t_new, dh), F32)],
        compiler_params=_params("parallel", "arbitrary"),
        name="attention_sample",
    )(qt, cache_k, cache_v, cache_k, cache_v, k_new, v_new, bias, bias_n, g_rows)
    a = o.reshape(bd, nh, t_new, dh).transpose(0, 2, 1, 3).reshape(bd, t_new, nh * dh)
    return a, wk, wv


REC_G = 4
REC_W = REC_G * HEAD_DIM


def _sigmoid_pair(x):
    e = jnp.exp(-jnp.abs(x))
    r = 1.0 / (1.0 + e)
    er = e * r
    pos = x >= 0
    return jnp.where(pos, r, er), jnp.where(pos, er, r)


def _hgrn_kernel(q_ref, f_ref, i_ref, gz_ref, lb_ref, g_ref, s0_ref, r_ref, sfin_ref, st_ref, *, chunk, valid):
    lblk = pl.program_id(2)
    n_chunks = q_ref.shape[0] // chunk

    @pl.when(lblk == 0)
    def _():
        for h in range(REC_G):
            st_ref[h] = s0_ref[h].T

    row = lax.broadcasted_iota(jnp.int32, (chunk, chunk), 0)
    col = lax.broadcasted_iota(jnp.int32, (chunk, chunk), 1)
    causal = row >= col
    tril = causal.astype(F32)
    row_ok = lax.broadcasted_iota(jnp.int32, (chunk, HEAD_DIM), 0) < valid

    def body(c, carry):
        rows = pl.ds(pl.multiple_of(c * chunk, chunk), chunk)
        for h in range(REC_G):
            cols = slice(h * HEAD_DIM, (h + 1) * HEAD_DIM)
            lb = lb_ref[:, cols]
            sig, sig_neg = _sigmoid_pair(f_ref[rows, cols])
            logf = jnp.log(lb + (1.0 - lb) * sig)
            kr = (1.0 - lb) * sig_neg
            if valid < chunk:
                logf = jnp.where(row_ok, logf, 0.0)
                kr = jnp.where(row_ok, kr, 0.0)
            qz = q_ref[rows, cols]
            q = qz * _sigmoid_pair(qz)[0]
            v = i_ref[rows, cols].astype(BF16)
            bcum = jnp.dot(tril, logf, precision=lax.Precision.HIGHEST, preferred_element_type=F32)
            blast = bcum[chunk - 1:chunk, :]
            q_dec = (q * jnp.exp(bcum)).astype(BF16)
            k_inv = (kr * jnp.exp(-bcum)).astype(BF16)
            k_dec = (kr * jnp.exp(blast - bcum)).astype(BF16)
            att = lax.dot_general(q_dec, k_inv, NT_DIMS, preferred_element_type=F32)
            att = jnp.where(causal, att, 0.0).astype(BF16)
            st = st_ref[h]
            o = (jnp.dot(att, v, preferred_element_type=F32)
                 + lax.dot_general(q_dec, st.astype(BF16), NT_DIMS, preferred_element_type=F32))
            st_ref[h] = st * jnp.exp(blast) + lax.dot_general(v, k_dec, TN_DIMS, preferred_element_type=F32)
            gz = gz_ref[rows, cols]
            gate = gz * _sigmoid_pair(gz)[0]
            r_ref[rows, cols] = (_rms(o, g_ref[:, cols]) * gate).astype(r_ref.dtype)
        return carry

    lax.fori_loop(0, n_chunks, body, 0)

    @pl.when(lblk == pl.num_programs(2) - 1)
    def _():
        for h in range(REC_G):
            sfin_ref[h] = st_ref[h].T


def hgrn(z, lb, g_rec, s0, *, chunk, valid, l_blk, out_dtype):
    b, l, _ = z.shape
    assert l % l_blk == 0 and l_blk % chunk == 0
    base = 3 * ATT_WIDTH // REC_W
    per = REC_WIDTH // REC_W
    zspec = lambda k: pl.BlockSpec((None, l_blk, REC_W), lambda bi, hg, li: (bi, li, base + k * per + hg))
    vec = pl.BlockSpec((1, REC_W), lambda bi, hg, li: (0, hg))
    st = pl.BlockSpec((None, REC_G, HEAD_DIM, HEAD_DIM), lambda bi, hg, li: (bi, hg, 0, 0))
    return pl.pallas_call(
        functools.partial(_hgrn_kernel, chunk=chunk, valid=valid),
        grid=(b, N_REC_HEADS // REC_G, l // l_blk),
        in_specs=[zspec(0), zspec(1), zspec(2), zspec(3), vec, vec, st],
        out_specs=[pl.BlockSpec((None, l_blk, REC_W), lambda bi, hg, li: (bi, li, hg)), st],
        out_shape=[jax.ShapeDtypeStruct((b, l, REC_WIDTH), out_dtype),
                   jax.ShapeDtypeStruct((b, N_REC_HEADS, HEAD_DIM, HEAD_DIM), F32)],
        scratch_shapes=[pltpu.VMEM((REC_G, HEAD_DIM, HEAD_DIM), F32)],
        compiler_params=_params("parallel", "parallel", "arbitrary"),
        name="hgrn",
    )(z, z, z, z, lb.reshape(1, REC_WIDTH), g_rec.reshape(1, REC_WIDTH), s0)


def _mem_attn_kernel(q_ref, k_ref, v_ref, o_ref):
    scale = HEAD_DIM ** -0.5
    for h in range(MEM_HEADS):
        cols = slice(h * HEAD_DIM, (h + 1) * HEAD_DIM)
        q = (q_ref[:, cols] * scale).astype(BF16)
        s = lax.dot_general(q, k_ref[:, cols].astype(BF16), NT_DIMS, preferred_element_type=F32)
        p = jnp.exp(s - jnp.max(s, axis=-1, keepdims=True))
        l = jnp.sum(p, axis=-1, keepdims=True)
        o = jnp.dot(p.astype(BF16), v_ref[:, cols].astype(BF16), preferred_element_type=F32)
        o_ref[:, cols] = (o / l).astype(o_ref.dtype)


def mem_attention(q, mk, mv, *, tq):
    b, t, w = q.shape
    tq = min(tq, t)
    return pl.pallas_call(
        _mem_attn_kernel,
        grid=(b, t // tq),
        in_specs=[pl.BlockSpec((None, tq, w), lambda bi, i: (bi, i, 0)),
                  pl.BlockSpec((None, N_MEM, w), lambda bi, i: (bi, 0, 0)),
                  pl.BlockSpec((None, N_MEM, w), lambda bi, i: (bi, 0, 0))],
        out_specs=pl.BlockSpec((None, tq, w), lambda bi, i: (bi, i, 0)),
        out_shape=jax.ShapeDtypeStruct((b, t, w), BF16),
        compiler_params=_params("parallel", "parallel"),
        name="mem_attention",
    )(q, mk, mv)


PEER_TM = 128


def _top16(s):
    n = s.shape[-1]
    lane = lax.broadcasted_iota(jnp.int32, s.shape, 1)
    lane_out = lax.broadcasted_iota(jnp.int32, (s.shape[0], LANES), 1)

    def body(it, carry):
        s, rank, vals = carry
        m = jnp.max(s, axis=-1, keepdims=True)
        idx = jnp.min(jnp.where(s == m, lane, n), axis=-1, keepdims=True)
        hit = lane == idx
        rank = jnp.where(hit, it.astype(F32), rank)
        vals = jnp.where(lane_out == it, m, vals)
        return jnp.where(hit, -jnp.inf, s), rank, vals

    _, rank, vals = lax.fori_loop(
        0, PEER_TOPK, body, (s, jnp.full(s.shape, float(PEER_TOPK), F32), jnp.zeros((s.shape[0], LANES), F32)))
    return vals, rank


def _peer_topk_kernel(q_ref, keys_ref, na_ref, e1z_ref, r2_ref, e2_ref):
    hi = lax.Precision.HIGHEST
    s1 = lax.dot_general(q_ref[:, 0:LANES], keys_ref[0], NT_DIMS, precision=hi, preferred_element_type=F32)
    s2 = lax.dot_general(q_ref[:, LANES:2 * LANES], keys_ref[1], NT_DIMS, precision=hi, preferred_element_type=F32)
    v1, r1 = _top16(s1)
    v2, r2 = _top16(s2)
    ncand = PEER_TOPK * PEER_TOPK
    src = lax.broadcasted_iota(jnp.int32, (LANES, ncand), 0)
    dst = lax.broadcasted_iota(jnp.int32, (LANES, ncand), 1)
    cand = (jnp.dot(v1, (src == dst // PEER_TOPK).astype(F32), precision=hi, preferred_element_type=F32)
            + jnp.dot(v2, (src == dst % PEER_TOPK).astype(F32), precision=hi, preferred_element_type=F32))
    rows = cand.shape[0]
    lane = lax.broadcasted_iota(jnp.int32, (rows, ncand), 1)

    def body(it, carry):
        cand, sel, top0, zsum = carry
        m = jnp.max(cand, axis=-1, keepdims=True)
        idx = jnp.min(jnp.where(cand == m, lane, ncand), axis=-1, keepdims=True)
        hit = lane == idx
        top0 = jnp.where(it == 0, m, top0)
        return jnp.where(hit, -jnp.inf, cand), jnp.where(hit, 1.0, sel), top0, zsum + jnp.exp(m - top0)

    zeros = jnp.zeros((rows, 1), F32)
    _, sel, _, zsum = lax.fori_loop(0, PEER_TOPK, body, (cand, jnp.zeros((rows, ncand), F32), zeros, zeros))
    gsrc = lax.broadcasted_iota(jnp.int32, (ncand, LANES), 0)
    gdst = lax.broadcasted_iota(jnp.int32, (ncand, LANES), 1)
    n_sel = jnp.dot(sel.astype(BF16), (gsrc // PEER_TOPK == gdst).astype(BF16), preferred_element_type=F32)
    lane128 = lax.broadcasted_iota(jnp.int32, (rows, LANES), 1)

    def spread(it, na):
        n_it = jnp.sum(jnp.where(lane128 == it, n_sel, 0.0), axis=-1, keepdims=True)
        return jnp.where(r1 == it.astype(F32), n_it, na)

    na = lax.fori_loop(0, PEER_TOPK, spread, jnp.zeros((rows, LANES), F32))
    na_ref[...] = na.astype(na_ref.dtype)
    r2_ref[...] = r2.astype(r2_ref.dtype)
    e1z_ref[...] = (jnp.exp(s1 - v1[:, 0:1]) / zsum).astype(e1z_ref.dtype)
    e2_ref[...] = jnp.exp(s2 - v2[:, 0:1])


def peer_topk(qry, sub_keys):
    m = qry.shape[0]
    tm = min(PEER_TM, m)
    width = PEER_HEADS * PEER_N_KEYS
    out_spec = pl.BlockSpec((tm, PEER_N_KEYS), lambda i, h: (i, h))
    return pl.pallas_call(
        _peer_topk_kernel,
        grid=(m // tm, PEER_HEADS),
        in_specs=[pl.BlockSpec((tm, 2 * PEER_N_KEYS), lambda i, h: (i, h)),
                  pl.BlockSpec((None, 2, PEER_N_KEYS, PEER_N_KEYS), lambda i, h: (h, 0, 0, 0))],
        out_specs=[out_spec, out_spec, out_spec, out_spec],
        out_shape=[jax.ShapeDtypeStruct((m, width), BF16), jax.ShapeDtypeStruct((m, width), BF16),
                   jax.ShapeDtypeStruct((m, width), BF16), jax.ShapeDtypeStruct((m, width), F32)],
        compiler_params=_params("parallel", "parallel"),
        name="peer_topk",
    )(qry, sub_keys)


PEER_TE = 256
PEER_SUB = PEER_TE // PEER_N_KEYS


def _peer_mix_kernel(h_ref, gn_ref, gf_ref, u_ref, v_ref, na_ref, e1z_ref, r2_ref, e2_ref, o_ref, xn_ref):
    e = pl.program_id(1)

    @pl.when(e == 0)
    def _():
        xn_ref[...] = _rms(h_ref[...], gn_ref[...]).astype(BF16)
        o_ref[...] = jnp.zeros(o_ref.shape, F32)

    hid = lax.dot_general(xn_ref[...], u_ref[...], NT_DIMS, preferred_element_type=F32)
    act = 0.5 * hid * (1.0 + lax.erf(hid * (2.0 ** -0.5)))
    key_row = lax.broadcasted_iota(jnp.int32, (PEER_N_KEYS, PEER_N_KEYS), 0)
    parts = []
    for sub in range(PEER_SUB):
        pick = (key_row == e * PEER_SUB + sub).astype(BF16)
        w = jnp.zeros((h_ref.shape[0], PEER_N_KEYS), F32)
        for hd in range(PEER_HEADS):
            cols = slice(hd * PEER_N_KEYS, (hd + 1) * PEER_N_KEYS)
            na = jnp.dot(na_ref[:, cols], pick, preferred_element_type=F32)
            e1z = jnp.dot(e1z_ref[:, cols], pick, preferred_element_type=F32)
            w = w + jnp.where(r2_ref[:, cols].astype(F32) < na, e2_ref[:, cols], 0.0) * e1z
        parts.append((w * act[:, sub * PEER_N_KEYS:(sub + 1) * PEER_N_KEYS]).astype(BF16))
    o_ref[...] += jnp.dot(jnp.concatenate(parts, axis=1), v_ref[...], preferred_element_type=F32)

    @pl.when(e == pl.num_programs(1) - 1)
    def _():
        o_ref[...] = _rms(h_ref[...] + o_ref[...], gf_ref[...])


def peer_mix(h, g_norm, g_final, u, v, na, e1z, r2, e2, *, tm):
    m, d = h.shape
    tm = min(tm, m)
    once = pl.Buffered(1)
    tok = lambda w: pl.BlockSpec((tm, w), lambda i, e: (i, 0), pipeline_mode=once)
    vec = pl.BlockSpec((1, d), lambda i, e: (0, 0))
    wgt = pl.BlockSpec((PEER_TE, d), lambda i, e: (e, 0))
    width = PEER_HEADS * PEER_N_KEYS
    return pl.pallas_call(
        _peer_mix_kernel,
        grid=(m // tm, PEER_N_EXPERTS // PEER_TE),
        in_specs=[tok(d), vec, vec, wgt, wgt, tok(width), tok(width), tok(width), tok(width)],
        out_specs=pl.BlockSpec((tm, d), lambda i, e: (i, 0)),
        out_shape=jax.ShapeDtypeStruct((m, d), F32),
        scratch_shapes=[pltpu.VMEM((tm, d), BF16)],
        compiler_params=_params("parallel", "arbitrary"),
        name="peer_mix",
    )(h, g_norm.reshape(1, d), g_final.reshape(1, d), u, v, na, e1z, r2, e2)


def _heads(x, n):
    return x.reshape(x.shape[0], x.shape[1], n, HEAD_DIM)


def kernel(x_prompt, x_sample, cache_win_k, cache_win_v, state_hgrn, cache_mem_k, cache_mem_v, mem_prompt, norm_mix, w_in, lb_table, g_att_out, g_rec_out, w_out, norm_mem_x, norm_mem_kv, w_mem_q, w_mem_k, w_mem_v, w_mem_o, norm_ffn, peer_w_query, peer_sub_keys, peer_u, peer_v, norm_final):
    bp, sp, d = x_prompt.shape
    bs, ts, _ = x_sample.shape
    depth = w_in.shape[0]
    assert depth == 1
    rope_p = _rope_tables(jnp.arange(sp, dtype=jnp.int32))
    rope_s = _rope_tables(jnp.tile(PAST_LEN + jnp.arange(ts, dtype=jnp.int32), bs))
    lower_bounds = jnp.cumsum(jax.nn.softmax(lb_table.astype(F32), axis=0), axis=0)
    hp = x_prompt.reshape(bp * sp, d)
    hs = x_sample.reshape(bs * ts, d)
    l = 0
    lb = lower_bounds[l]
    w_in_b, w_out_b = w_in[l].astype(BF16), w_out[l].astype(BF16)

    zp = norm_matmul(hp, norm_mix[l], w_in_b, rope=rope_p, rope_cols=2 * ATT_WIDTH).reshape(bp, sp, IN_COLS)
    a_p = attention_prompt(zp, g_att_out[l])
    r_p, st_p = hgrn(zp, lb, g_rec_out[l], jnp.zeros((bp, N_REC_HEADS, HEAD_DIM, HEAD_DIM), F32),
                     chunk=32, valid=32, l_blk=1024, out_dtype=BF16)
    mix_p = jnp.concatenate([a_p, r_p], axis=-1).reshape(bp * sp, d)
    hp = matmul_residual(mix_p, w_out_b, hp)
    wk_p = _heads(zp[:, sp - WIN_BUF:, ATT_WIDTH:2 * ATT_WIDTH], N_ATT_HEADS)
    wv_p = _heads(zp[:, sp - WIN_BUF:, 2 * ATT_WIDTH:3 * ATT_WIDTH], N_ATT_HEADS)

    zs = norm_matmul(hs, norm_mix[l], w_in_b, rope=rope_s, rope_cols=2 * ATT_WIDTH).reshape(bs, ts, IN_COLS)
    a_s, wk_s, wv_s = attention_sample(_heads(zs[..., :ATT_WIDTH], N_ATT_HEADS),
                                       _heads(zs[..., ATT_WIDTH:2 * ATT_WIDTH], N_ATT_HEADS),
                                       _heads(zs[..., 2 * ATT_WIDTH:3 * ATT_WIDTH], N_ATT_HEADS),
                                       cache_win_k[l], cache_win_v[l], g_att_out[l])
    pad_t = 16
    zs_pad = jnp.pad(zs, ((0, 0), (0, pad_t - ts), (0, 0)))
    r_s, st_s = hgrn(zs_pad, lb, g_rec_out[l], state_hgrn[l], chunk=pad_t, valid=ts, l_blk=pad_t, out_dtype=F32)
    mix_s = jnp.concatenate([a_s, r_s[:, :ts]], axis=-1).reshape(bs * ts, d).astype(BF16)
    hs = matmul_residual(mix_s, w_out_b, hs)

    w_kv = jnp.concatenate([w_mem_k[l], w_mem_v[l]], axis=1).astype(BF16)
    mkv = norm_matmul(mem_prompt.reshape(bp * N_MEM, d), norm_mem_kv[l], w_kv)
    mk_p = mkv[:, :MEM_WIDTH].reshape(bp, N_MEM, MEM_WIDTH)
    mv_p = mkv[:, MEM_WIDTH:].reshape(bp, N_MEM, MEM_WIDTH)
    w_q_b, w_o_b = w_mem_q[l].astype(BF16), w_mem_o[l].astype(BF16)
    qm_p = norm_matmul(hp, norm_mem_x[l], w_q_b).reshape(bp, sp, MEM_WIDTH)
    hp = matmul_residual(mem_attention(qm_p, mk_p, mv_p, tq=512).reshape(bp * sp, MEM_WIDTH), w_o_b, hp)
    qm_s = norm_matmul(hs, norm_mem_x[l], w_q_b).reshape(bs, ts, MEM_WIDTH)
    qm_s = jnp.pad(qm_s, ((0, 0), (0, pad_t - ts), (0, 0)))
    om_s = mem_attention(qm_s, cache_mem_k[l].reshape(bs, N_MEM, MEM_WIDTH), cache_mem_v[l].reshape(bs, N_MEM, MEM_WIDTH), tq=pad_t)
    hs = matmul_residual(om_s[:, :ts].reshape(bs * ts, MEM_WIDTH), w_o_b, hs)

    w_pq = peer_w_query[l].astype(BF16)
    u_b, v_b = peer_u[l].astype(BF16), peer_v[l].astype(BF16)
    sel_p = peer_topk(norm_matmul(hp, norm_ffn[l], w_pq), peer_sub_keys[l])
    y_p = peer_mix(hp, norm_ffn[l], norm_final, u_b, v_b, *sel_p, tm=512)
    sel_s = peer_topk(norm_matmul(hs, norm_ffn[l], w_pq), peer_sub_keys[l])
    y_s = peer_mix(hs, norm_ffn[l], norm_final, u_b, v_b, *sel_s, tm=128)

    return (y_p.reshape(bp, sp, d), y_s.reshape(bs, ts, d),
            wk_p[None], wv_p[None], st_p[None],
            _heads(mk_p, MEM_HEADS)[None], _heads(mv_p, MEM_HEADS)[None],
            wk_s[None], wv_s[None], st_s[None])
```

```python
import functools

import jax
import jax.numpy as jnp
from jax import lax
from jax.experimental import pallas as pl
from jax.experimental.pallas import tpu as pltpu

F32 = jnp.float32
BF16 = jnp.bfloat16

D_MODEL = 4096
HEAD_DIM = 128
ATT_WIDTH = 2048
REC_WIDTH = 2048
N_ATT_HEADS = 16
N_REC_HEADS = 16
IN_COLS = 3 * ATT_WIDTH + 4 * REC_WIDTH
DILATED_PATTERNS = ((128, 1), (512, 4), (2048, 16))
WIN_BUF = 2048
PAST_LEN = 8192
ATT_SCALE = HEAD_DIM ** -0.5
ROT_HALF = HEAD_DIM // 8
ROPE_THETA = 500000.0
N_MEM = 256
MEM_HEADS = 4
MEM_WIDTH = 512
PEER_N_KEYS = 128
PEER_N_EXPERTS = PEER_N_KEYS * PEER_N_KEYS
PEER_HEADS = 8
PEER_TOPK = 16
EPS = 1e-6
NEG = -1e30

LANES = 128
VMEM_LIMIT = 56 * 1024 * 1024

NT_DIMS = (((1,), (1,)), ((), ()))
TN_DIMS = (((0,), (0,)), ((), ()))


def _params(*sem):
    return pltpu.CompilerParams(dimension_semantics=sem, vmem_limit_bytes=VMEM_LIMIT)


def _rms(x, g):
    return x * lax.rsqrt(jnp.mean(x * x, axis=-1, keepdims=True) + EPS) * g


def _norm_matmul_kernel(*refs, rope_tiles, heads_per_tile):
    if rope_tiles:
        x_ref, g_ref, w_ref, cos_ref, sa_ref, sb_ref, o_ref, xn_ref = refs
    else:
        x_ref, g_ref, w_ref, o_ref, xn_ref = refs
    j = pl.program_id(1)

    @pl.when(j == 0)
    def _():
        xn_ref[...] = _rms(x_ref[...], g_ref[...]).astype(BF16)

    z = jnp.dot(xn_ref[...], w_ref[...], preferred_element_type=F32)
    if rope_tiles:
        @pl.when(j < rope_tiles)
        def _():
            cos, sa, sb = cos_ref[...], sa_ref[...], sb_ref[...]
            for h in range(heads_per_tile):
                zh = z[:, h * LANES:(h + 1) * LANES]
                o_ref[:, h * LANES:(h + 1) * LANES] = (
                    zh * cos + pltpu.roll(zh, LANES - ROT_HALF, 1) * sa + pltpu.roll(zh, ROT_HALF, 1) * sb)

        @pl.when(j >= rope_tiles)
        def _():
            o_ref[...] = z
    else:
        o_ref[...] = z.astype(o_ref.dtype)


def norm_matmul(x, g, w, *, rope=None, rope_cols=0, out_dtype=F32, tm=512, tn=512, return_normed=False):
    m, k = x.shape
    n = w.shape[1]
    tm, tn = min(tm, m), min(tn, n)
    assert m % tm == 0 and n % tn == 0 and rope_cols % tn == 0
    in_specs = [pl.BlockSpec((tm, k), lambda i, j: (i, 0)),
                pl.BlockSpec((1, k), lambda i, j: (0, 0)),
                pl.BlockSpec((k, tn), lambda i, j: (0, j))]
    args = [x, g.reshape(1, k), w]
    rope_tiles = 0
    if rope is not None:
        cos, sa, sb = rope
        period = cos.shape[0] // tm
        rope_tiles = rope_cols // tn
        spec = pl.BlockSpec((tm, LANES), lambda i, j: (i % period, 0))
        in_specs += [spec, spec, spec]
        args += [cos, sa, sb]
    out_specs = [pl.BlockSpec((tm, tn), lambda i, j: (i, j))]
    out_shape = [jax.ShapeDtypeStruct((m, n), out_dtype)]
    scratch = [pltpu.VMEM((tm, k), BF16)]
    if return_normed:
        out_specs.append(pl.BlockSpec((tm, k), lambda i, j: (i, 0)))
        out_shape.append(jax.ShapeDtypeStruct((m, k), BF16))
        scratch = []
    out = pl.pallas_call(
        functools.partial(_norm_matmul_kernel, rope_tiles=rope_tiles, heads_per_tile=tn // LANES),
        grid=(m // tm, n // tn),
        in_specs=in_specs,
        out_specs=out_specs,
        out_shape=out_shape,
        scratch_shapes=scratch,
        compiler_params=_params("parallel", "arbitrary"),
        name="norm_matmul",
    )(*args)
    return out if return_normed else out[0]


def _rope_tables(pos):
    inv = ROPE_THETA ** (-jnp.arange(ROT_HALF, dtype=F32) / ROT_HALF)
    ang = pos.astype(F32)[:, None] * inv[None, :]
    cos, sin = jnp.cos(ang), jnp.sin(ang)
    t = pos.shape[0]
    rest = LANES - 2 * ROT_HALF
    cos_t = jnp.concatenate([cos, cos, jnp.ones((t, rest), F32)], axis=1)
    sa = jnp.concatenate([-sin, jnp.zeros((t, LANES - ROT_HALF), F32)], axis=1)
    sb = jnp.concatenate([jnp.zeros((t, ROT_HALF), F32), sin, jnp.zeros((t, rest), F32)], axis=1)
    return cos_t, sa, sb


def _matmul_residual_kernel(a_ref, w_ref, r_ref, o_ref):
    o_ref[...] = r_ref[...] + jnp.dot(a_ref[...], w_ref[...], preferred_element_type=F32)


def matmul_residual(a, w, res, *, tm=512, tn=512):
    m, k = a.shape
    n = w.shape[1]
    tm, tn = min(tm, m), min(tn, n)
    assert m % tm == 0 and n % tn == 0
    return pl.pallas_call(
        _matmul_residual_kernel,
        grid=(m // tm, n // tn),
        in_specs=[pl.BlockSpec((tm, k), lambda i, j: (i, 0)),
                  pl.BlockSpec((k, tn), lambda i, j: (0, j)),
                  pl.BlockSpec((tm, tn), lambda i, j: (i, j))],
        out_specs=pl.BlockSpec((tm, tn), lambda i, j: (i, j)),
        out_shape=jax.ShapeDtypeStruct((m, n), F32),
        compiler_params=_params("parallel", "parallel"),
        name="matmul_residual",
    )(a, w, res)


ATT_TQ = 256
ATT_BACK = WIN_BUF // ATT_TQ
ATT_WIN = WIN_BUF + ATT_TQ


def _distance_multiplicity(delta):
    c = jnp.zeros(delta.shape, F32)
    for window, dil in DILATED_PATTERNS:
        c = c + ((delta >= 0) & (delta <= window) & (delta % dil == 0)).astype(F32)
    return c


def _log_multiplicity(delta):
    c = _distance_multiplicity(delta)
    return jnp.where(c > 0, jnp.log(jnp.maximum(c, 1.0)), NEG)


def _attn_prompt_kernel(q_ref, k_ref, v_ref, g_ref, bias_ref, o_ref, kb_ref, vb_ref, s_ref):
    i = pl.program_id(2)

    @pl.when(i == 0)
    def _():
        kb_ref[0:WIN_BUF] = jnp.zeros((WIN_BUF, HEAD_DIM), BF16)
        vb_ref[0:WIN_BUF] = jnp.zeros((WIN_BUF, HEAD_DIM), BF16)
        kb_ref[WIN_BUF:] = k_ref[...].astype(BF16)
        vb_ref[WIN_BUF:] = v_ref[...].astype(BF16)

    q = (q_ref[...] * ATT_SCALE).astype(BF16)
    win = pl.ds(pl.multiple_of(i * ATT_TQ, ATT_TQ), ATT_WIN)
    s_ref[...] = lax.dot_general(q, kb_ref[win, :], NT_DIMS, preferred_element_type=F32) + bias_ref[...]

    @pl.when(i < ATT_BACK)
    def _():
        col = lax.broadcasted_iota(jnp.int32, (1, ATT_WIN), 1)
        s_ref[...] = jnp.where(col >= WIN_BUF - i * ATT_TQ, s_ref[...], NEG)

    s = s_ref[...]
    p = jnp.exp(s - jnp.max(s, axis=-1, keepdims=True))
    l = jnp.sum(p, axis=-1, keepdims=True)
    o = jnp.dot(p.astype(BF16), vb_ref[win, :], preferred_element_type=F32) / l
    o_ref[...] = _rms(o, g_ref[...]).astype(o_ref.dtype)


def attention_prompt(z, g_att):
    b, s, _ = z.shape
    rq = jnp.arange(ATT_TQ)[:, None]
    col = jnp.arange(ATT_WIN)[None, :]
    bias = _log_multiplicity(WIN_BUF + rq - col)
    kcol, vcol = ATT_WIDTH // HEAD_DIM, 2 * ATT_WIDTH // HEAD_DIM
    return pl.pallas_call(
        _attn_prompt_kernel,
        grid=(b, N_ATT_HEADS, s // ATT_TQ),
        in_specs=[pl.BlockSpec((None, ATT_TQ, HEAD_DIM), lambda bi, h, i: (bi, i, h)),
                  pl.BlockSpec((None, s, HEAD_DIM), lambda bi, h, i: (bi, 0, kcol + h)),
                  pl.BlockSpec((None, s, HEAD_DIM), lambda bi, h, i: (bi, 0, vcol + h)),
                  pl.BlockSpec((1, HEAD_DIM), lambda bi, h, i: (0, h)),
                  pl.BlockSpec((ATT_TQ, ATT_WIN), lambda bi, h, i: (0, 0))],
        out_specs=pl.BlockSpec((None, ATT_TQ, HEAD_DIM), lambda bi, h, i: (bi, i, h)),
        out_shape=jax.ShapeDtypeStruct((b, s, ATT_WIDTH), BF16),
        scratch_shapes=[pltpu.VMEM((WIN_BUF + s, HEAD_DIM), BF16), pltpu.VMEM((WIN_BUF + s, HEAD_DIM), BF16),
                        pltpu.VMEM((ATT_TQ, ATT_WIN), F32)],
        compiler_params=_params("parallel", "parallel", "arbitrary"),
        name="attention_prompt",
    )(z, z, z, g_att.reshape(1, ATT_WIDTH), bias)


SMP_TB = 256
SMP_ROWS = SMP_TB * N_ATT_HEADS


def _attn_sample_kernel(q_ref, kc_ref, vc_ref, kx_ref, vx_ref, kn_ref, vn_ref, bias_ref, biasn_ref, g_ref,
                        wk_ref, wv_ref, o_ref, m_ref, l_ref, acc_ref, *, t_new):
    kblk = pl.program_id(1)
    last = pl.num_programs(1) - 1

    @pl.when(kblk == 0)
    def _():
        m_ref[...] = jnp.full(m_ref.shape, NEG, F32)
        l_ref[...] = jnp.zeros(l_ref.shape, F32)
        acc_ref[...] = jnp.zeros(acc_ref.shape, F32)

    wk_ref[0:SMP_TB - t_new] = kc_ref[t_new:SMP_TB]
    wv_ref[0:SMP_TB - t_new] = vc_ref[t_new:SMP_TB]

    @pl.when(kblk < last)
    def _():
        wk_ref[SMP_TB - t_new:SMP_TB] = kx_ref[...]
        wv_ref[SMP_TB - t_new:SMP_TB] = vx_ref[...]

    @pl.when(kblk == last)
    def _():
        wk_ref[SMP_TB - t_new:SMP_TB] = kn_ref[...]
        wv_ref[SMP_TB - t_new:SMP_TB] = vn_ref[...]

    q = q_ref[...]

    def update(k2, v2, bias):
        s = lax.dot_general(q, k2.astype(BF16), NT_DIMS, preferred_element_type=F32) + bias
        m_old = m_ref[...]
        m_new = jnp.maximum(m_old, jnp.max(s, axis=-1, keepdims=True))
        alpha = jnp.exp(m_old - m_new)
        p = jnp.exp(s - m_new)
        l_ref[...] = alpha * l_ref[...] + jnp.sum(p, axis=-1, keepdims=True)
        acc_ref[...] = alpha * acc_ref[...] + jnp.dot(p.astype(BF16), v2.astype(BF16), preferred_element_type=F32)
        m_ref[...] = m_new

    update(kc_ref[...].reshape(SMP_ROWS, HEAD_DIM), vc_ref[...].reshape(SMP_ROWS, HEAD_DIM), bias_ref[...])

    @pl.when(kblk == last)
    def _():
        update(kn_ref[...].reshape(t_new * N_ATT_HEADS, HEAD_DIM), vn_ref[...].reshape(t_new * N_ATT_HEADS, HEAD_DIM),
               biasn_ref[...])
        o_ref[...] = _rms(acc_ref[...] / l_ref[...], g_ref[...])


def attention_sample(q, k_new, v_new, cache_k, cache_v, g_att):
    bd, t_new, nh, dh = q.shape
    nblk = WIN_BUF // SMP_TB
    qt = (q * ATT_SCALE).transpose(0, 2, 1, 3).reshape(bd, nh * t_new, dh).astype(BF16)
    hq = jnp.arange(nh * t_new)[:, None] // t_new
    jq = jnp.arange(nh * t_new)[:, None] % t_new
    def make_bias(tok, hk):
        return jnp.where(hq == hk, _log_multiplicity(WIN_BUF + jq - tok), NEG)
    cols = jnp.arange(WIN_BUF * nh)[None, :]
    bias = make_bias(cols // nh, cols % nh).reshape(nh * t_new, nblk, SMP_ROWS).transpose(1, 0, 2)
    cols_n = jnp.arange(t_new * nh)[None, :]
    bias_n = make_bias(WIN_BUF + cols_n // nh, cols_n % nh)
    g_rows = jnp.repeat(g_att.reshape(nh, dh), t_new, axis=0)
    sub = SMP_TB // t_new
    cache_blk = pl.BlockSpec((None, SMP_TB, nh, dh), lambda b, k: (b, k, 0, 0))
    next_blk = pl.BlockSpec((None, t_new, nh, dh), lambda b, k: (b, jnp.minimum((k + 1) * sub, WIN_BUF // t_new - 1), 0, 0))
    new_blk = pl.BlockSpec((None, t_new, nh, dh), lambda b, k: (b, 0, 0, 0))
    wk, wv, o = pl.pallas_call(
        functools.partial(_attn_sample_kernel, t_new=t_new),
        grid=(bd, nblk),
        in_specs=[pl.BlockSpec((None, nh * t_new, dh), lambda b, k: (b, 0, 0)),
                  cache_blk, cache_blk, next_blk, next_blk, new_blk, new_blk,
                  pl.BlockSpec((None, nh * t_new, SMP_ROWS), lambda b, k: (k, 0, 0)),
                  pl.BlockSpec((nh * t_new, nh * t_new), lambda b, k: (0, 0)),
                  pl.BlockSpec((nh * t_new, dh), lambda b, k: (0, 0))],
        out_specs=[cache_blk, cache_blk, pl.BlockSpec((None, nh * t_new, dh), lambda b, k: (b, 0, 0))],
        out_shape=[jax.ShapeDtypeStruct(cache_k.shape, F32), jax.ShapeDtypeStruct(cache_v.shape, F32),
                   jax.ShapeDtypeStruct((bd, nh * t_new, dh), F32)],
        scratch_shapes=[pltpu.VMEM((nh * t_new, 1), F32), pltpu.VMEM((nh * t_new, 1), F32),
                        pltpu.VMEM((nh * t_new, dh), F32)],
        compiler_params=_params("parallel", "arbitrary"),
        name="attention_sample",
    )(qt, cache_k, cache_v, cache_k, cache_v, k_new, v_new, bias, bias_n, g_rows)
    a = o.reshape(bd, nh, t_new, dh).transpose(0, 2, 1, 3).reshape(bd, t_new, nh * dh)
    return a, wk, wv


REC_G = 4
REC_W = REC_G * HEAD_DIM


def _sigmoid_pair(x):
    e = jnp.exp(-jnp.abs(x))
    r = 1.0 / (1.0 + e)
    er = e * r
    pos = x >= 0
    return jnp.where(pos, r, er), jnp.where(pos, er, r)


REC_GROUP = 128
SUBLANES = 8


def _hgrn_kernel(q_ref, f_ref, i_ref, gz_ref, lb_ref, g_ref, s0_ref, r_ref, sfin_ref,
                 st_ref, oin_ref, qd_ref, u_ref, dec_ref, *, chunk, valid):
    lblk = pl.program_id(2)
    l_blk = q_ref.shape[0]
    grp = min(REC_GROUP, l_blk)
    per = grp // chunk

    @pl.when(lblk == 0)
    def _():
        for h in range(REC_G):
            st_ref[h] = s0_ref[h].T

    row = lax.broadcasted_iota(jnp.int32, (grp, grp), 0)
    col = lax.broadcasted_iota(jnp.int32, (grp, grp), 1)
    same_chunk_causal = (row // chunk == col // chunk) & (row >= col)
    pos = lax.broadcasted_iota(jnp.int32, (grp, HEAD_DIM), 0) % chunk

    def group_body(g, carry):
        rows = pl.ds(pl.multiple_of(g * grp, grp), grp)
        for h in range(REC_G):
            cols = slice(h * HEAD_DIM, (h + 1) * HEAD_DIM)
            lb = lb_ref[:, cols]
            sig, sig_neg = _sigmoid_pair(f_ref[rows, cols])
            logf = jnp.log(lb + (1.0 - lb) * sig)
            kr = (1.0 - lb) * sig_neg
            if valid < chunk:
                logf = jnp.where(pos < valid, logf, 0.0)
                kr = jnp.where(pos < valid, kr, 0.0)
            bcum = logf
            d = 1
            while d < chunk:
                bcum = bcum + jnp.where(pos >= d, pltpu.roll(bcum, d, 0), 0.0)
                d *= 2
            last = bcum.reshape(per, chunk, HEAD_DIM)[:, chunk - 1:chunk, :]
            blast = jnp.broadcast_to(last, (per, chunk, HEAD_DIM)).reshape(grp, HEAD_DIM)
            qz = q_ref[rows, cols]
            q = qz * _sigmoid_pair(qz)[0]
            v = i_ref[rows, cols].astype(BF16)
            q_dec = (q * jnp.exp(bcum)).astype(BF16)
            k_inv = (kr * jnp.exp(-bcum)).astype(BF16)
            k_dec = (kr * jnp.exp(blast - bcum)).astype(BF16)
            att = lax.dot_general(q_dec, k_inv, NT_DIMS, preferred_element_type=F32)
            att = jnp.where(same_chunk_causal, att, 0.0).astype(BF16)
            oin_ref[h, rows, :] = jnp.dot(att, v, preferred_element_type=F32)
            qd_ref[h, rows, :] = q_dec
            for c in range(per):
                cr = slice(c * chunk, (c + 1) * chunk)
                u_ref[h, g * per + c] = lax.dot_general(v[cr], k_dec[cr], TN_DIMS, preferred_element_type=F32)
                dec_ref[h, g * per + c] = jnp.broadcast_to(jnp.exp(last[c]), (SUBLANES, HEAD_DIM))
        return carry

    lax.fori_loop(0, l_blk // grp, group_body, 0)

    def chunk_body(c, carry):
        rows = pl.ds(pl.multiple_of(c * chunk, chunk), chunk)
        for h in range(REC_G):
            cols = slice(h * HEAD_DIM, (h + 1) * HEAD_DIM)
            st = st_ref[h]
            o = oin_ref[h, rows, :] + lax.dot_general(qd_ref[h, rows, :], st.astype(BF16), NT_DIMS,
                                                      preferred_element_type=F32)
            st3 = st.reshape(HEAD_DIM // SUBLANES, SUBLANES, HEAD_DIM) * dec_ref[h, c][None]
            st_ref[h] = st3.reshape(HEAD_DIM, HEAD_DIM) + u_ref[h, c]
            gz = gz_ref[rows, cols]
            gate = gz * _sigmoid_pair(gz)[0]
            r_ref[rows, cols] = (_rms(o, g_ref[:, cols]) * gate).astype(r_ref.dtype)
        return carry

    lax.fori_loop(0, l_blk // chunk, chunk_body, 0, unroll=min(4, l_blk // chunk))

    @pl.when(lblk == pl.num_programs(2) - 1)
    def _():
        for h in range(REC_G):
            sfin_ref[h] = st_ref[h].T


def hgrn(z, lb, g_rec, s0, *, chunk, valid, l_blk, out_dtype):
    b, l, _ = z.shape
    assert l % l_blk == 0 and l_blk % chunk == 0
    base = 3 * ATT_WIDTH // REC_W
    per = REC_WIDTH // REC_W
    zspec = lambda k: pl.BlockSpec((None, l_blk, REC_W), lambda bi, hg, li: (bi, li, base + k * per + hg))
    vec = pl.BlockSpec((1, REC_W), lambda bi, hg, li: (0, hg))
    st = pl.BlockSpec((None, REC_G, HEAD_DIM, HEAD_DIM), lambda bi, hg, li: (bi, hg, 0, 0))
    return pl.pallas_call(
        functools.partial(_hgrn_kernel, chunk=chunk, valid=valid),
        grid=(b, N_REC_HEADS // REC_G, l // l_blk),
        in_specs=[zspec(0), zspec(1), zspec(2), zspec(3), vec, vec, st],
        out_specs=[pl.BlockSpec((None, l_blk, REC_W), lambda bi, hg, li: (bi, li, hg)), st],
        out_shape=[jax.ShapeDtypeStruct((b, l, REC_WIDTH), out_dtype),
                   jax.ShapeDtypeStruct((b, N_REC_HEADS, HEAD_DIM, HEAD_DIM), F32)],
        scratch_shapes=[pltpu.VMEM((REC_G, HEAD_DIM, HEAD_DIM), F32),
                        pltpu.VMEM((REC_G, l_blk, HEAD_DIM), F32),
                        pltpu.VMEM((REC_G, l_blk, HEAD_DIM), BF16),
                        pltpu.VMEM((REC_G, l_blk // chunk, HEAD_DIM, HEAD_DIM), F32),
                        pltpu.VMEM((REC_G, l_blk // chunk, SUBLANES, HEAD_DIM), F32)],
        compiler_params=_params("parallel", "parallel", "arbitrary"),
        name="hgrn",
    )(z, z, z, z, lb.reshape(1, REC_WIDTH), g_rec.reshape(1, REC_WIDTH), s0)


def _mem_attn_kernel(q_ref, k_ref, v_ref, o_ref):
    scale = HEAD_DIM ** -0.5
    for h in range(MEM_HEADS):
        cols = slice(h * HEAD_DIM, (h + 1) * HEAD_DIM)
        q = (q_ref[:, cols] * scale).astype(BF16)
        s = lax.dot_general(q, k_ref[:, cols].astype(BF16), NT_DIMS, preferred_element_type=F32)
        p = jnp.exp(s - jnp.max(s, axis=-1, keepdims=True))
        l = jnp.sum(p, axis=-1, keepdims=True)
        o = jnp.dot(p.astype(BF16), v_ref[:, cols].astype(BF16), preferred_element_type=F32)
        o_ref[:, cols] = (o / l).astype(o_ref.dtype)


def mem_attention(q, mk, mv, *, tq):
    b, t, w = q.shape
    tq = min(tq, t)
    return pl.pallas_call(
        _mem_attn_kernel,
        grid=(b, t // tq),
        in_specs=[pl.BlockSpec((None, tq, w), lambda bi, i: (bi, i, 0)),
                  pl.BlockSpec((None, N_MEM, w), lambda bi, i: (bi, 0, 0)),
                  pl.BlockSpec((None, N_MEM, w), lambda bi, i: (bi, 0, 0))],
        out_specs=pl.BlockSpec((None, tq, w), lambda bi, i: (bi, i, 0)),
        out_shape=jax.ShapeDtypeStruct((b, t, w), BF16),
        compiler_params=_params("parallel", "parallel"),
        name="mem_attention",
    )(q, mk, mv)


PEER_TT = 256
PEER_MARK = 1e30
PEER_WIDE = 8


def _first_max(x, order, sentinel):
    m = jnp.max(x, axis=0, keepdims=True)
    return m, jnp.min(jnp.where(x == m, order, sentinel), axis=0, keepdims=True)


def _top16_cols(scores):
    n, t = scores[0].shape
    row = lax.broadcasted_iota(jnp.int32, (n, t), 0)
    row_out = lax.broadcasted_iota(jnp.int32, (PEER_TOPK, t), 0)

    def body(it, carry):
        mark = (it.astype(F32) + 1.0) * -PEER_MARK
        out = []
        for s, vals in carry:
            m, first = _first_max(s, row, n)
            out.append((jnp.where(row == first, mark, s), jnp.where(row_out == it, m, vals)))
        return tuple(out)

    done = lax.fori_loop(0, PEER_TOPK, body, tuple((s, jnp.zeros((PEER_TOPK, t), F32)) for s in scores))
    return [(vals, jnp.where(s <= -0.5 * PEER_MARK, jnp.floor(s * (-1.0 / PEER_MARK) - 0.5), float(PEER_TOPK)))
            for s, vals in done]


def _candidates(v1, v2):
    parts = [v1[0:1, :] + v2]
    parts += [v1[r:r + 1, :] + v2[0:PEER_WIDE, :] for r in range(1, PEER_WIDE)]
    parts.append(v1[PEER_WIDE:, :] + v2[0:1, :])
    return jnp.concatenate(parts, axis=0)


def _peer_topk_kernel(q_ref, keys_ref, na_ref, e1z_ref, r2_ref, e2_ref):
    hi = lax.Precision.HIGHEST
    s1_all = lax.dot_general(keys_ref[0], q_ref[:, 0:LANES], NT_DIMS, precision=hi, preferred_element_type=F32)
    s2_all = lax.dot_general(keys_ref[1], q_ref[:, LANES:2 * LANES], NT_DIMS, precision=hi,
                             preferred_element_type=F32)
    n_chains = q_ref.shape[0] // LANES
    stage1 = []
    for c in range(n_chains):
        lanes = slice(c * LANES, (c + 1) * LANES)
        stage1.append(_top16_cols((s1_all[:, lanes], s2_all[:, lanes])))

    head = PEER_TOPK + (PEER_WIDE - 1) * PEER_WIDE
    n_cand = head + PEER_TOPK - PEER_WIDE
    j = lax.broadcasted_iota(jnp.int32, (n_cand, LANES), 0)
    k = j - PEER_TOPK
    order = jnp.where(j < PEER_TOPK, j,
                      jnp.where(j < head, (1 + k // PEER_WIDE) * PEER_TOPK + k % PEER_WIDE,
                                (j - head + PEER_WIDE) * PEER_TOPK))

    def final_body(it, carry):
        out = []
        for cand, top0, zsum in carry:
            m, first = _first_max(cand, order, PEER_TOPK * PEER_TOPK)
            top0 = jnp.where(it == 0, m, top0)
            out.append((jnp.where(order == first, -jnp.inf, cand), top0, zsum + jnp.exp(m - top0)))
        return tuple(out)

    zeros = jnp.zeros((1, LANES), F32)
    final = lax.fori_loop(0, PEER_TOPK, final_body,
                          tuple((_candidates(v1, v2), zeros, zeros) for (v1, _), (v2, _) in stage1))

    for c in range(n_chains):
        lanes = slice(c * LANES, (c + 1) * LANES)
        (v1, r1), (v2, r2) = stage1[c]
        cand, _, zsum = final[c]
        sel = jnp.where(cand == -jnp.inf, 1.0, 0.0)
        na = jnp.zeros(r1.shape, F32)
        for r in range(PEER_TOPK):
            if r == 0:
                n_r = jnp.sum(sel[0:PEER_TOPK, :], axis=0, keepdims=True)
            elif r < PEER_WIDE:
                lo = PEER_TOPK + (r - 1) * PEER_WIDE
                n_r = jnp.sum(sel[lo:lo + PEER_WIDE, :], axis=0, keepdims=True)
            else:
                n_r = sel[head + r - PEER_WIDE:head + r - PEER_WIDE + 1, :]
            na = jnp.where(r1 == float(r), n_r, na)
        na_ref[:, lanes] = na
        r2_ref[:, lanes] = r2.astype(r2_ref.dtype)
        e1z_ref[:, lanes] = jnp.exp(s1_all[:, lanes] - v1[0:1, :]) / zsum
        e2_ref[:, lanes] = jnp.exp(s2_all[:, lanes] - v2[0:1, :]).astype(e2_ref.dtype)


def peer_topk(qry, sub_keys):
    m = qry.shape[0]
    tt = min(PEER_TT, m)
    out_spec = pl.BlockSpec((None, PEER_N_KEYS, tt), lambda i, h: (h, 0, i))
    shape = (PEER_HEADS, PEER_N_KEYS, m)
    return pl.pallas_call(
        _peer_topk_kernel,
        grid=(m // tt, PEER_HEADS),
        in_specs=[pl.BlockSpec((tt, 2 * PEER_N_KEYS), lambda i, h: (i, h)),
                  pl.BlockSpec((None, 2, PEER_N_KEYS, PEER_N_KEYS), lambda i, h: (h, 0, 0, 0))],
        out_specs=[out_spec, out_spec, out_spec, out_spec],
        out_shape=[jax.ShapeDtypeStruct(shape, F32), jax.ShapeDtypeStruct(shape, F32),
                   jax.ShapeDtypeStruct(shape, BF16), jax.ShapeDtypeStruct(shape, BF16)],
        compiler_params=_params("parallel", "parallel"),
        name="peer_topk",
    )(qry, sub_keys)


PEER_TE = 256
PEER_SUB = PEER_TE // PEER_N_KEYS


PEER_ROWS = 16

_ERF_P = (-2.72614225801306e-10, 2.77068142495902e-08, -2.10102402082508e-06, -5.69250639462346e-05,
          -7.34990630326855e-04, -2.95459980854025e-03, -1.60960333262415e-02)
_ERF_Q = (-1.45660718464996e-05, -2.13374055278905e-04, -1.68282697438203e-03, -7.37332916720468e-03,
          -1.42647390514189e-02)


def _gelu(x):
    z = jnp.clip(x * (2.0 ** -0.5), -4.0, 4.0)
    z2 = z * z
    p = _ERF_P[0] * z2 + _ERF_P[1]
    for c in _ERF_P[2:]:
        p = p * z2 + c
    q = _ERF_Q[0] * z2 + _ERF_Q[1]
    for c in _ERF_Q[2:]:
        q = q * z2 + c
    return 0.5 * x * (1.0 + z * p / q)


def _peer_mix_kernel(xt_ref, h_ref, gf_ref, u_ref, v_ref, na_ref, e1z_ref, r2_ref, e2_ref, o_ref, wt_ref):
    e = pl.program_id(1)
    n_tiles = pl.num_programs(1) - 1
    tm = xt_ref.shape[1]
    groups = PEER_N_KEYS // PEER_ROWS

    @pl.when(e == 0)
    def _():
        o_ref[...] = jnp.zeros(o_ref.shape, F32)
        wt_ref[...] = jnp.zeros(wt_ref.shape, BF16)

    o_ref[...] += lax.dot_general(wt_ref[...], v_ref[...], TN_DIMS, preferred_element_type=F32)

    tile = jnp.minimum(e, n_tiles - 1)
    hid = jnp.dot(u_ref[...], xt_ref[...], preferred_element_type=F32)
    act = _gelu(hid).astype(BF16)
    parts = []
    for sub in range(PEER_SUB):
        a = tile * PEER_SUB + sub
        w = jnp.zeros((groups, PEER_ROWS, tm), BF16)
        for hd in range(PEER_HEADS):
            na = jnp.broadcast_to(na_ref[hd, pl.ds(a, 1), :], (PEER_ROWS, tm)).astype(BF16)
            e1z = jnp.broadcast_to(e1z_ref[hd, pl.ds(a, 1), :], (PEER_ROWS, tm)).astype(BF16)
            r2 = r2_ref[hd].reshape(groups, PEER_ROWS, tm)
            e2 = e2_ref[hd].reshape(groups, PEER_ROWS, tm)
            w = w + jnp.where(r2 < na[None], e2, jnp.zeros_like(e2)) * e1z[None]
        w = w.reshape(PEER_N_KEYS, tm) * act[sub * PEER_N_KEYS:(sub + 1) * PEER_N_KEYS, :]
        parts.append(w)
    wt_ref[...] = jnp.concatenate(parts, axis=0)

    @pl.when(e == n_tiles)
    def _():
        o_ref[...] = _rms(h_ref[...] + o_ref[...], gf_ref[...])


def peer_mix(xt, h, g_final, u, v, na, e1z, r2, e2, *, tm):
    m, d = h.shape
    tm = min(tm, m)
    once = pl.Buffered(1)
    sel = pl.BlockSpec((PEER_HEADS, PEER_N_KEYS, tm), lambda i, e: (0, 0, i), pipeline_mode=once)
    n_tiles = PEER_N_EXPERTS // PEER_TE
    return pl.pallas_call(
        _peer_mix_kernel,
        grid=(m // tm, n_tiles + 1),
        in_specs=[pl.BlockSpec((d, tm), lambda i, e: (0, i), pipeline_mode=once),
                  pl.BlockSpec((tm, d), lambda i, e: (i, 0), pipeline_mode=once),
                  pl.BlockSpec((1, d), lambda i, e: (0, 0)),
                  pl.BlockSpec((PEER_TE, d), lambda i, e: (jnp.minimum(e, n_tiles - 1), 0)),
                  pl.BlockSpec((PEER_TE, d), lambda i, e: (jnp.maximum(e - 1, 0), 0)),
                  sel, sel, sel, sel],
        out_specs=pl.BlockSpec((tm, d), lambda i, e: (i, 0)),
        out_shape=jax.ShapeDtypeStruct((m, d), F32),
        scratch_shapes=[pltpu.VMEM((PEER_TE, tm), BF16)],
        compiler_params=_params("parallel", "arbitrary"),
        name="peer_mix",
    )(xt, h, g_final.reshape(1, d), u, v, na, e1z, r2, e2)


def _heads(x, n):
    return x.reshape(x.shape[0], x.shape[1], n, HEAD_DIM)


def kernel(x_prompt, x_sample, cache_win_k, cache_win_v, state_hgrn, cache_mem_k, cache_mem_v, mem_prompt, norm_mix, w_in, lb_table, g_att_out, g_rec_out, w_out, norm_mem_x, norm_mem_kv, w_mem_q, w_mem_k, w_mem_v, w_mem_o, norm_ffn, peer_w_query, peer_sub_keys, peer_u, peer_v, norm_final):
    bp, sp, d = x_prompt.shape
    bs, ts, _ = x_sample.shape
    depth = w_in.shape[0]
    assert depth == 1
    rope_p = _rope_tables(jnp.arange(sp, dtype=jnp.int32))
    rope_s = _rope_tables(jnp.tile(PAST_LEN + jnp.arange(ts, dtype=jnp.int32), bs))
    lower_bounds = jnp.cumsum(jax.nn.softmax(lb_table.astype(F32), axis=0), axis=0)
    hp = x_prompt.reshape(bp * sp, d)
    hs = x_sample.reshape(bs * ts, d)
    l = 0
    lb = lower_bounds[l]
    w_in_b, w_out_b = w_in[l].astype(BF16), w_out[l].astype(BF16)

    zp = norm_matmul(hp, norm_mix[l], w_in_b, rope=rope_p, rope_cols=2 * ATT_WIDTH).reshape(bp, sp, IN_COLS)
    a_p = attention_prompt(zp, g_att_out[l])
    r_p, st_p = hgrn(zp, lb, g_rec_out[l], jnp.zeros((bp, N_REC_HEADS, HEAD_DIM, HEAD_DIM), F32),
                     chunk=32, valid=32, l_blk=1024, out_dtype=BF16)
    mix_p = jnp.concatenate([a_p, r_p], axis=-1).reshape(bp * sp, d)
    hp = matmul_residual(mix_p, w_out_b, hp)
    wk_p = _heads(zp[:, sp - WIN_BUF:, ATT_WIDTH:2 * ATT_WIDTH], N_ATT_HEADS)
    wv_p = _heads(zp[:, sp - WIN_BUF:, 2 * ATT_WIDTH:3 * ATT_WIDTH], N_ATT_HEADS)

    zs = norm_matmul(hs, norm_mix[l], w_in_b, rope=rope_s, rope_cols=2 * ATT_WIDTH).reshape(bs, ts, IN_COLS)
    a_s, wk_s, wv_s = attention_sample(_heads(zs[..., :ATT_WIDTH], N_ATT_HEADS),
                                       _heads(zs[..., ATT_WIDTH:2 * ATT_WIDTH], N_ATT_HEADS),
                                       _heads(zs[..., 2 * ATT_WIDTH:3 * ATT_WIDTH], N_ATT_HEADS),
                                       cache_win_k[l], cache_win_v[l], g_att_out[l])
    pad_t = 16
    zs_pad = jnp.pad(zs, ((0, 0), (0, pad_t - ts), (0, 0)))
    r_s, st_s = hgrn(zs_pad, lb, g_rec_out[l], state_hgrn[l], chunk=pad_t, valid=ts, l_blk=pad_t, out_dtype=F32)
    mix_s = jnp.concatenate([a_s, r_s[:, :ts]], axis=-1).reshape(bs * ts, d).astype(BF16)
    hs = matmul_residual(mix_s, w_out_b, hs)

    w_kv = jnp.concatenate([w_mem_k[l], w_mem_v[l]], axis=1).astype(BF16)
    mkv = norm_matmul(mem_prompt.reshape(bp * N_MEM, d), norm_mem_kv[l], w_kv)
    mk_p = mkv[:, :MEM_WIDTH].reshape(bp, N_MEM, MEM_WIDTH)
    mv_p = mkv[:, MEM_WIDTH:].reshape(bp, N_MEM, MEM_WIDTH)
    w_q_b, w_o_b = w_mem_q[l].astype(BF16), w_mem_o[l].astype(BF16)
    qm_p = norm_matmul(hp, norm_mem_x[l], w_q_b).reshape(bp, sp, MEM_WIDTH)
    hp = matmul_residual(mem_attention(qm_p, mk_p, mv_p, tq=512).reshape(bp * sp, MEM_WIDTH), w_o_b, hp)
    qm_s = norm_matmul(hs, norm_mem_x[l], w_q_b).reshape(bs, ts, MEM_WIDTH)
    qm_s = jnp.pad(qm_s, ((0, 0), (0, pad_t - ts), (0, 0)))
    om_s = mem_attention(qm_s, cache_mem_k[l].reshape(bs, N_MEM, MEM_WIDTH), cache_mem_v[l].reshape(bs, N_MEM, MEM_WIDTH), tq=pad_t)
    hs = matmul_residual(om_s[:, :ts].reshape(bs * ts, MEM_WIDTH), w_o_b, hs)

    w_pq = peer_w_query[l].astype(BF16)
    u_b, v_b = peer_u[l].astype(BF16), peer_v[l].astype(BF16)
    qry_p, xn_p = norm_matmul(hp, norm_ffn[l], w_pq, return_normed=True)
    y_p = peer_mix(xn_p.T, hp, norm_final, u_b, v_b, *peer_topk(qry_p, peer_sub_keys[l]), tm=512)
    qry_s, xn_s = norm_matmul(hs, norm_ffn[l], w_pq, return_normed=True)
    y_s = peer_mix(xn_s.T, hs, norm_final, u_b, v_b, *peer_topk(qry_s, peer_sub_keys[l]), tm=128)

    return (y_p.reshape(bp, sp, d), y_s.reshape(bs, ts, d),
            wk_p[None], wv_p[None], st_p[None],
            _heads(mk_p, MEM_HEADS)[None], _heads(mv_p, MEM_HEADS)[None],
            wk_s[None], wv_s[None], st_s[None])
```

```python
import functools

import jax
import jax.numpy as jnp
from jax import lax
from jax.experimental import pallas as pl
from jax.experimental.pallas import tpu as pltpu

F32 = jnp.float32
BF16 = jnp.bfloat16

D_MODEL = 4096
HEAD_DIM = 128
ATT_WIDTH = 2048
REC_WIDTH = 2048
N_ATT_HEADS = 16
N_REC_HEADS = 16
IN_COLS = 3 * ATT_WIDTH + 4 * REC_WIDTH
DILATED_PATTERNS = ((128, 1), (512, 4), (2048, 16))
WIN_BUF = 2048
PAST_LEN = 8192
ATT_SCALE = HEAD_DIM ** -0.5
ROT_HALF = HEAD_DIM // 8
ROPE_THETA = 500000.0
N_MEM = 256
MEM_HEADS = 4
MEM_WIDTH = 512
PEER_N_KEYS = 128
PEER_N_EXPERTS = PEER_N_KEYS * PEER_N_KEYS
PEER_HEADS = 8
PEER_TOPK = 16
EPS = 1e-6
NEG = -1e30

LANES = 128
VMEM_LIMIT = 56 * 1024 * 1024

NT_DIMS = (((1,), (1,)), ((), ()))
TN_DIMS = (((0,), (0,)), ((), ()))


def _params(*sem):
    return pltpu.CompilerParams(dimension_semantics=sem, vmem_limit_bytes=VMEM_LIMIT)


def _rms(x, g):
    return x * lax.rsqrt(jnp.mean(x * x, axis=-1, keepdims=True) + EPS) * g


def _norm_matmul_kernel(*refs, rope_tiles, heads_per_tile):
    if rope_tiles:
        x_ref, g_ref, w_ref, cos_ref, sa_ref, sb_ref, o_ref, xn_ref = refs
    else:
        x_ref, g_ref, w_ref, o_ref, xn_ref = refs
    j = pl.program_id(1)

    @pl.when(j == 0)
    def _():
        xn_ref[...] = _rms(x_ref[...], g_ref[...]).astype(BF16)

    o_ref[...] = jnp.dot(xn_ref[...], w_ref[...], preferred_element_type=F32).astype(o_ref.dtype)
    if rope_tiles:
        @pl.when(j < rope_tiles)
        def _():
            cos, sa, sb = cos_ref[...], sa_ref[...], sb_ref[...]
            for h in range(heads_per_tile):
                zh = o_ref[:, h * LANES:(h + 1) * LANES]
                o_ref[:, h * LANES:(h + 1) * LANES] = (
                    zh * cos + pltpu.roll(zh, LANES - ROT_HALF, 1) * sa + pltpu.roll(zh, ROT_HALF, 1) * sb)


def norm_matmul(x, g, w, *, rope=None, rope_cols=0, out_dtype=F32, tm=512, tn=512, return_normed=False):
    m, k = x.shape
    n = w.shape[1]
    tm, tn = min(tm, m), min(tn, n)
    assert m % tm == 0 and n % tn == 0 and rope_cols % tn == 0
    in_specs = [pl.BlockSpec((tm, k), lambda i, j: (i, 0)),
                pl.BlockSpec((1, k), lambda i, j: (0, 0)),
                pl.BlockSpec((k, tn), lambda i, j: (0, j))]
    args = [x, g.reshape(1, k), w]
    rope_tiles = 0
    if rope is not None:
        cos, sa, sb = rope
        period = cos.shape[0] // tm
        rope_tiles = rope_cols // tn
        spec = pl.BlockSpec((tm, LANES), lambda i, j: (i % period, 0))
        in_specs += [spec, spec, spec]
        args += [cos, sa, sb]
    out_specs = [pl.BlockSpec((tm, tn), lambda i, j: (i, j))]
    out_shape = [jax.ShapeDtypeStruct((m, n), out_dtype)]
    scratch = [pltpu.VMEM((tm, k), BF16)]
    if return_normed:
        out_specs.append(pl.BlockSpec((tm, k), lambda i, j: (i, 0)))
        out_shape.append(jax.ShapeDtypeStruct((m, k), BF16))
        scratch = []
    out = pl.pallas_call(
        functools.partial(_norm_matmul_kernel, rope_tiles=rope_tiles, heads_per_tile=tn // LANES),
        grid=(m // tm, n // tn),
        in_specs=in_specs,
        out_specs=out_specs,
        out_shape=out_shape,
        scratch_shapes=scratch,
        compiler_params=_params("parallel", "arbitrary"),
        name="norm_matmul",
    )(*args)
    return out if return_normed else out[0]


def _rope_tables(pos):
    inv = ROPE_THETA ** (-jnp.arange(ROT_HALF, dtype=F32) / ROT_HALF)
    ang = pos.astype(F32)[:, None] * inv[None, :]
    cos, sin = jnp.cos(ang), jnp.sin(ang)
    t = pos.shape[0]
    rest = LANES - 2 * ROT_HALF
    cos_t = jnp.concatenate([cos, cos, jnp.ones((t, rest), F32)], axis=1)
    sa = jnp.concatenate([-sin, jnp.zeros((t, LANES - ROT_HALF), F32)], axis=1)
    sb = jnp.concatenate([jnp.zeros((t, ROT_HALF), F32), sin, jnp.zeros((t, rest), F32)], axis=1)
    return cos_t, sa, sb


def _matmul_residual_kernel(a_ref, w_ref, r_ref, o_ref):
    o_ref[...] = r_ref[...] + jnp.dot(a_ref[...], w_ref[...], preferred_element_type=F32)


def matmul_residual(a, w, res, *, tm=512, tn=512):
    m, k = a.shape
    n = w.shape[1]
    tm, tn = min(tm, m), min(tn, n)
    assert m % tm == 0 and n % tn == 0
    return pl.pallas_call(
        _matmul_residual_kernel,
        grid=(m // tm, n // tn),
        in_specs=[pl.BlockSpec((tm, k), lambda i, j: (i, 0)),
                  pl.BlockSpec((k, tn), lambda i, j: (0, j)),
                  pl.BlockSpec((tm, tn), lambda i, j: (i, j))],
        out_specs=pl.BlockSpec((tm, tn), lambda i, j: (i, j)),
        out_shape=jax.ShapeDtypeStruct((m, n), F32),
        compiler_params=_params("parallel", "parallel"),
        name="matmul_residual",
    )(a, w, res)


ATT_TQ = 256
ATT_BACK = WIN_BUF // ATT_TQ
ATT_WIN = WIN_BUF + ATT_TQ


def _distance_multiplicity(delta):
    c = jnp.zeros(delta.shape, F32)
    for window, dil in DILATED_PATTERNS:
        c = c + ((delta >= 0) & (delta <= window) & (delta % dil == 0)).astype(F32)
    return c


def _log_multiplicity(delta):
    c = _distance_multiplicity(delta)
    return jnp.where(c > 0, jnp.log(jnp.maximum(c, 1.0)), NEG)


def _attn_prompt_kernel(q_ref, k_ref, v_ref, g_ref, bias_ref, o_ref, kb_ref, vb_ref, s_ref):
    i = pl.program_id(2)

    @pl.when(i == 0)
    def _():
        kb_ref[0:WIN_BUF] = jnp.zeros((WIN_BUF, HEAD_DIM), BF16)
        vb_ref[0:WIN_BUF] = jnp.zeros((WIN_BUF, HEAD_DIM), BF16)
        kb_ref[WIN_BUF:] = k_ref[...].astype(BF16)
        vb_ref[WIN_BUF:] = v_ref[...].astype(BF16)

    q = (q_ref[...] * ATT_SCALE).astype(BF16)
    win = pl.ds(pl.multiple_of(i * ATT_TQ, ATT_TQ), ATT_WIN)
    s_ref[...] = lax.dot_general(q, kb_ref[win, :], NT_DIMS, preferred_element_type=F32) + bias_ref[...]

    @pl.when(i < ATT_BACK)
    def _():
        col = lax.broadcasted_iota(jnp.int32, (1, ATT_WIN), 1)
        s_ref[...] = jnp.where(col >= WIN_BUF - i * ATT_TQ, s_ref[...], NEG)

    s = s_ref[...]
    p = jnp.exp(s - jnp.max(s, axis=-1, keepdims=True))
    l = jnp.sum(p, axis=-1, keepdims=True)
    o = jnp.dot(p.astype(BF16), vb_ref[win, :], preferred_element_type=F32) / l
    o_ref[...] = _rms(o, g_ref[...]).astype(o_ref.dtype)


def attention_prompt(z, g_att):
    b, s, _ = z.shape
    rq = jnp.arange(ATT_TQ)[:, None]
    col = jnp.arange(ATT_WIN)[None, :]
    bias = _log_multiplicity(WIN_BUF + rq - col)
    kcol, vcol = ATT_WIDTH // HEAD_DIM, 2 * ATT_WIDTH // HEAD_DIM
    return pl.pallas_call(
        _attn_prompt_kernel,
        grid=(b, N_ATT_HEADS, s // ATT_TQ),
        in_specs=[pl.BlockSpec((None, ATT_TQ, HEAD_DIM), lambda bi, h, i: (bi, i, h)),
                  pl.BlockSpec((None, s, HEAD_DIM), lambda bi, h, i: (bi, 0, kcol + h)),
                  pl.BlockSpec((None, s, HEAD_DIM), lambda bi, h, i: (bi, 0, vcol + h)),
                  pl.BlockSpec((1, HEAD_DIM), lambda bi, h, i: (0, h)),
                  pl.BlockSpec((ATT_TQ, ATT_WIN), lambda bi, h, i: (0, 0))],
        out_specs=pl.BlockSpec((None, ATT_TQ, HEAD_DIM), lambda bi, h, i: (bi, i, h)),
        out_shape=jax.ShapeDtypeStruct((b, s, ATT_WIDTH), BF16),
        scratch_shapes=[pltpu.VMEM((WIN_BUF + s, HEAD_DIM), BF16), pltpu.VMEM((WIN_BUF + s, HEAD_DIM), BF16),
                        pltpu.VMEM((ATT_TQ, ATT_WIN), F32)],
        compiler_params=_params("parallel", "parallel", "arbitrary"),
        name="attention_prompt",
    )(z, z, z, g_att.reshape(1, ATT_WIDTH), bias)


SMP_TB = 512
SMP_ROWS = SMP_TB * N_ATT_HEADS


def _attn_sample_kernel(q_ref, kc_ref, vc_ref, kx_ref, vx_ref, kn_ref, vn_ref, bias_ref, biasn_ref, g_ref,
                        wk_ref, wv_ref, o_ref, m_ref, l_ref, acc_ref, *, t_new):
    kblk = pl.program_id(1)
    last = pl.num_programs(1) - 1

    @pl.when(kblk == 0)
    def _():
        m_ref[...] = jnp.full(m_ref.shape, NEG, F32)
        l_ref[...] = jnp.zeros(l_ref.shape, F32)
        acc_ref[...] = jnp.zeros(acc_ref.shape, F32)

    wk_ref[0:SMP_TB - t_new] = kc_ref[t_new:SMP_TB]
    wv_ref[0:SMP_TB - t_new] = vc_ref[t_new:SMP_TB]

    @pl.when(kblk < last)
    def _():
        wk_ref[SMP_TB - t_new:SMP_TB] = kx_ref[...]
        wv_ref[SMP_TB - t_new:SMP_TB] = vx_ref[...]

    @pl.when(kblk == last)
    def _():
        wk_ref[SMP_TB - t_new:SMP_TB] = kn_ref[...]
        wv_ref[SMP_TB - t_new:SMP_TB] = vn_ref[...]

    q = q_ref[...]

    def update(k2, v2, bias):
        s = lax.dot_general(q, k2.astype(BF16), NT_DIMS, preferred_element_type=F32) + bias
        m_old = m_ref[...]
        m_new = jnp.maximum(m_old, jnp.max(s, axis=-1, keepdims=True))
        alpha = jnp.exp(m_old - m_new)
        p = jnp.exp(s - m_new)
        l_ref[...] = alpha * l_ref[...] + jnp.sum(p, axis=-1, keepdims=True)
        acc_ref[...] = alpha * acc_ref[...] + jnp.dot(p.astype(BF16), v2.astype(BF16), preferred_element_type=F32)
        m_ref[...] = m_new

    update(kc_ref[...].reshape(SMP_ROWS, HEAD_DIM), vc_ref[...].reshape(SMP_ROWS, HEAD_DIM), bias_ref[...])

    @pl.when(kblk == last)
    def _():
        update(kn_ref[...].reshape(t_new * N_ATT_HEADS, HEAD_DIM), vn_ref[...].reshape(t_new * N_ATT_HEADS, HEAD_DIM),
               biasn_ref[...])
        o_ref[...] = _rms(acc_ref[...] / l_ref[...], g_ref[...])


def attention_sample(q, k_new, v_new, cache_k, cache_v, g_att):
    bd, t_new, nh, dh = q.shape
    nblk = WIN_BUF // SMP_TB
    qt = (q * ATT_SCALE).transpose(0, 2, 1, 3).reshape(bd, nh * t_new, dh).astype(BF16)
    hq = jnp.arange(nh * t_new)[:, None] // t_new
    jq = jnp.arange(nh * t_new)[:, None] % t_new
    def make_bias(tok, hk):
        return jnp.where(hq == hk, _log_multiplicity(WIN_BUF + jq - tok), NEG)
    cols = jnp.arange(WIN_BUF * nh)[None, :]
    bias = make_bias(cols // nh, cols % nh).reshape(nh * t_new, nblk, SMP_ROWS).transpose(1, 0, 2)
    cols_n = jnp.arange(t_new * nh)[None, :]
    bias_n = make_bias(WIN_BUF + cols_n // nh, cols_n % nh)
    g_rows = jnp.repeat(g_att.reshape(nh, dh), t_new, axis=0)
    sub = SMP_TB // t_new
    cache_blk = pl.BlockSpec((None, SMP_TB, nh, dh), lambda b, k: (b, k, 0, 0))
    next_blk = pl.BlockSpec((None, t_new, nh, dh), lambda b, k: (b, jnp.minimum((k + 1) * sub, WIN_BUF // t_new - 1), 0, 0))
    new_blk = pl.BlockSpec((None, t_new, nh, dh), lambda b, k: (b, 0, 0, 0))
    wk, wv, o = pl.pallas_call(
        functools.partial(_attn_sample_kernel, t_new=t_new),
        grid=(bd, nblk),
        in_specs=[pl.BlockSpec((None, nh * t_new, dh), lambda b, k: (b, 0, 0)),
                  cache_blk, cache_blk, next_blk, next_blk, new_blk, new_blk,
                  pl.BlockSpec((None, nh * t_new, SMP_ROWS), lambda b, k: (k, 0, 0)),
                  pl.BlockSpec((nh * t_new, nh * t_new), lambda b, k: (0, 0)),
                  pl.BlockSpec((nh * t_new, dh), lambda b, k: (0, 0))],
        out_specs=[cache_blk, cache_blk, pl.BlockSpec((None, nh * t_new, dh), lambda b, k: (b, 0, 0))],
        out_shape=[jax.ShapeDtypeStruct(cache_k.shape, F32), jax.ShapeDtypeStruct(cache_v.shape, F32),
                   jax.ShapeDtypeStruct((bd, nh * t_new, dh), F32)],
        scratch_shapes=[pltpu.VMEM((nh * t_new, 1), F32), pltpu.VMEM((nh * t_new, 1), F32),
                        pltpu.VMEM((nh * t_new, dh), F32)],
        compiler_params=_params("parallel", "arbitrary"),
        name="attention_sample",
    )(qt, cache_k, cache_v, cache_k, cache_v, k_new, v_new, bias, bias_n, g_rows)
    a = o.reshape(bd, nh, t_new, dh).transpose(0, 2, 1, 3).reshape(bd, t_new, nh * dh)
    return a, wk, wv


REC_G = 4
REC_W = REC_G * HEAD_DIM


def _sigmoid_pair(x):
    e = jnp.exp(-jnp.abs(x))
    r = 1.0 / (1.0 + e)
    er = e * r
    pos = x >= 0
    return jnp.where(pos, r, er), jnp.where(pos, er, r)


REC_GROUP = 128
SUBLANES = 8


def _hgrn_kernel(q_ref, f_ref, i_ref, gz_ref, lb_ref, g_ref, s0_ref, r_ref, sfin_ref,
                 st_ref, oin_ref, qd_ref, u_ref, dec_ref, *, chunk, valid):
    lblk = pl.program_id(2)
    l_blk = q_ref.shape[0]
    grp = min(REC_GROUP, l_blk)
    per = grp // chunk

    @pl.when(lblk == 0)
    def _():
        for h in range(REC_G):
            st_ref[h] = s0_ref[h].T

    row = lax.broadcasted_iota(jnp.int32, (grp, grp), 0)
    col = lax.broadcasted_iota(jnp.int32, (grp, grp), 1)
    same_chunk_causal = (row // chunk == col // chunk) & (row >= col)
    pos = lax.broadcasted_iota(jnp.int32, (grp, HEAD_DIM), 0) % chunk

    def group_body(g, carry):
        rows = pl.ds(pl.multiple_of(g * grp, grp), grp)
        for h in range(REC_G):
            cols = slice(h * HEAD_DIM, (h + 1) * HEAD_DIM)
            lb = lb_ref[:, cols]
            sig, sig_neg = _sigmoid_pair(f_ref[rows, cols])
            logf = jnp.log(lb + (1.0 - lb) * sig)
            kr = (1.0 - lb) * sig_neg
            if valid < chunk:
                logf = jnp.where(pos < valid, logf, 0.0)
                kr = jnp.where(pos < valid, kr, 0.0)
            bcum = logf
            d = 1
            while d < chunk:
                bcum = bcum + jnp.where(pos >= d, pltpu.roll(bcum, d, 0), 0.0)
                d *= 2
            last = bcum.reshape(per, chunk, HEAD_DIM)[:, chunk - 1:chunk, :]
            blast = jnp.broadcast_to(last, (per, chunk, HEAD_DIM)).reshape(grp, HEAD_DIM)
            qz = q_ref[rows, cols]
            q = qz * _sigmoid_pair(qz)[0]
            v = i_ref[rows, cols].astype(BF16)
            q_dec = (q * jnp.exp(bcum)).astype(BF16)
            k_inv = (kr * jnp.exp(-bcum)).astype(BF16)
            k_dec = (kr * jnp.exp(blast - bcum)).astype(BF16)
            att = lax.dot_general(q_dec, k_inv, NT_DIMS, preferred_element_type=F32)
            att = jnp.where(same_chunk_causal, att, 0.0).astype(BF16)
            oin_ref[h, rows, :] = jnp.dot(att, v, preferred_element_type=F32)
            qd_ref[h, rows, :] = q_dec
            for c in range(per):
                cr = slice(c * chunk, (c + 1) * chunk)
                u_ref[h, g * per + c] = lax.dot_general(v[cr], k_dec[cr], TN_DIMS, preferred_element_type=F32)
                dec_ref[h, g * per + c] = jnp.broadcast_to(jnp.exp(last[c]), (SUBLANES, HEAD_DIM))
        return carry

    lax.fori_loop(0, l_blk // grp, group_body, 0)

    def chunk_body(c, carry):
        rows = pl.ds(pl.multiple_of(c * chunk, chunk), chunk)
        for h in range(REC_G):
            cols = slice(h * HEAD_DIM, (h + 1) * HEAD_DIM)
            st = st_ref[h]
            o = oin_ref[h, rows, :] + lax.dot_general(qd_ref[h, rows, :], st.astype(BF16), NT_DIMS,
                                                      preferred_element_type=F32)
            st3 = st.reshape(HEAD_DIM // SUBLANES, SUBLANES, HEAD_DIM) * dec_ref[h, c][None]
            st_ref[h] = st3.reshape(HEAD_DIM, HEAD_DIM) + u_ref[h, c]
            gz = gz_ref[rows, cols]
            gate = gz * _sigmoid_pair(gz)[0]
            r_ref[rows, cols] = (_rms(o, g_ref[:, cols]) * gate).astype(r_ref.dtype)
        return carry

    lax.fori_loop(0, l_blk // chunk, chunk_body, 0, unroll=min(4, l_blk // chunk))

    @pl.when(lblk == pl.num_programs(2) - 1)
    def _():
        for h in range(REC_G):
            sfin_ref[h] = st_ref[h].T


def hgrn(z, lb, g_rec, s0, *, chunk, valid, l_blk, out_dtype):
    b, l, _ = z.shape
    assert l % l_blk == 0 and l_blk % chunk == 0
    base = 3 * ATT_WIDTH // REC_W
    per = REC_WIDTH // REC_W
    zspec = lambda k: pl.BlockSpec((None, l_blk, REC_W), lambda bi, hg, li: (bi, li, base + k * per + hg))
    vec = pl.BlockSpec((1, REC_W), lambda bi, hg, li: (0, hg))
    st = pl.BlockSpec((None, REC_G, HEAD_DIM, HEAD_DIM), lambda bi, hg, li: (bi, hg, 0, 0))
    return pl.pallas_call(
        functools.partial(_hgrn_kernel, chunk=chunk, valid=valid),
        grid=(b, N_REC_HEADS // REC_G, l // l_blk),
        in_specs=[zspec(0), zspec(1), zspec(2), zspec(3), vec, vec, st],
        out_specs=[pl.BlockSpec((None, l_blk, REC_W), lambda bi, hg, li: (bi, li, hg)), st],
        out_shape=[jax.ShapeDtypeStruct((b, l, REC_WIDTH), out_dtype),
                   jax.ShapeDtypeStruct((b, N_REC_HEADS, HEAD_DIM, HEAD_DIM), F32)],
        scratch_shapes=[pltpu.VMEM((REC_G, HEAD_DIM, HEAD_DIM), F32),
                        pltpu.VMEM((REC_G, l_blk, HEAD_DIM), F32),
                        pltpu.VMEM((REC_G, l_blk, HEAD_DIM), BF16),
                        pltpu.VMEM((REC_G, l_blk // chunk, HEAD_DIM, HEAD_DIM), F32),
                        pltpu.VMEM((REC_G, l_blk // chunk, SUBLANES, HEAD_DIM), F32)],
        compiler_params=_params("parallel", "parallel", "arbitrary"),
        name="hgrn",
    )(z, z, z, z, lb.reshape(1, REC_WIDTH), g_rec.reshape(1, REC_WIDTH), s0)


def _mem_attn_kernel(q_ref, k_ref, v_ref, o_ref):
    scale = HEAD_DIM ** -0.5
    for h in range(MEM_HEADS):
        cols = slice(h * HEAD_DIM, (h + 1) * HEAD_DIM)
        q = (q_ref[:, cols] * scale).astype(BF16)
        s = lax.dot_general(q, k_ref[:, cols].astype(BF16), NT_DIMS, preferred_element_type=F32)
        p = jnp.exp(s - jnp.max(s, axis=-1, keepdims=True))
        l = jnp.sum(p, axis=-1, keepdims=True)
        o = jnp.dot(p.astype(BF16), v_ref[:, cols].astype(BF16), preferred_element_type=F32)
        o_ref[:, cols] = (o / l).astype(o_ref.dtype)


def mem_attention(q, mk, mv, *, tq):
    b, t, w = q.shape
    tq = min(tq, t)
    return pl.pallas_call(
        _mem_attn_kernel,
        grid=(b, t // tq),
        in_specs=[pl.BlockSpec((None, tq, w), lambda bi, i: (bi, i, 0)),
                  pl.BlockSpec((None, N_MEM, w), lambda bi, i: (bi, 0, 0)),
                  pl.BlockSpec((None, N_MEM, w), lambda bi, i: (bi, 0, 0))],
        out_specs=pl.BlockSpec((None, tq, w), lambda bi, i: (bi, i, 0)),
        out_shape=jax.ShapeDtypeStruct((b, t, w), BF16),
        compiler_params=_params("parallel", "parallel"),
        name="mem_attention",
    )(q, mk, mv)


PEER_TT = 256
PEER_MARK = 1e30
PEER_WIDE = 8


def _first_max(x, order, sentinel):
    m = jnp.max(x, axis=0, keepdims=True)
    return m, jnp.min(jnp.where(x == m, order, sentinel), axis=0, keepdims=True)


def _top16_cols(scores):
    n, t = scores[0].shape
    row = lax.broadcasted_iota(jnp.int32, (n, t), 0)
    row_out = lax.broadcasted_iota(jnp.int32, (PEER_TOPK, t), 0)

    def body(it, carry):
        mark = (lax.convert_element_type(it, F32) + 1.0) * -PEER_MARK
        out = []
        for s, vals in carry:
            m, first = _first_max(s, row, n)
            out.append((jnp.where(row == first, mark, s), jnp.where(row_out == it, m, vals)))
        return tuple(out)

    done = lax.fori_loop(0, PEER_TOPK, body, tuple((s, jnp.zeros((PEER_TOPK, t), F32)) for s in scores))
    return [(vals, jnp.where(s <= -0.5 * PEER_MARK, jnp.floor(s * (-1.0 / PEER_MARK) - 0.5), float(PEER_TOPK)))
            for s, vals in done]


def _candidates(v1, v2):
    parts = [v1[0:1, :] + v2]
    parts += [v1[r:r + 1, :] + v2[0:PEER_WIDE, :] for r in range(1, PEER_WIDE)]
    parts.append(v1[PEER_WIDE:, :] + v2[0:1, :])
    return jnp.concatenate(parts, axis=0)


def _peer_topk_kernel(q_ref, keys_ref, na_ref, e1z_ref, r2_ref, e2_ref):
    hi = lax.Precision.HIGHEST
    s1_all = lax.dot_general(keys_ref[0], q_ref[:, 0:LANES], NT_DIMS, precision=hi, preferred_element_type=F32)
    s2_all = lax.dot_general(keys_ref[1], q_ref[:, LANES:2 * LANES], NT_DIMS, precision=hi,
                             preferred_element_type=F32)
    n_chains = q_ref.shape[0] // LANES
    stage1 = []
    for c in range(n_chains):
        lanes = slice(c * LANES, (c + 1) * LANES)
        stage1.append(_top16_cols((s1_all[:, lanes], s2_all[:, lanes])))

    head = PEER_TOPK + (PEER_WIDE - 1) * PEER_WIDE
    n_cand = head + PEER_TOPK - PEER_WIDE
    j = lax.broadcasted_iota(jnp.int32, (n_cand, LANES), 0)
    k = j - PEER_TOPK
    order = jnp.where(j < PEER_TOPK, j,
                      jnp.where(j < head, (1 + k // PEER_WIDE) * PEER_TOPK + k % PEER_WIDE,
                                (j - head + PEER_WIDE) * PEER_TOPK))

    def final_body(it, carry):
        out = []
        for cand, top0, zsum in carry:
            m, first = _first_max(cand, order, PEER_TOPK * PEER_TOPK)
            top0 = jnp.where(it == 0, m, top0)
            out.append((jnp.where(order == first, -jnp.inf, cand), top0, zsum + jnp.exp(m - top0)))
        return tuple(out)

    zeros = jnp.zeros((1, LANES), F32)
    final = lax.fori_loop(0, PEER_TOPK, final_body,
                          tuple((_candidates(v1, v2), zeros, zeros) for (v1, _), (v2, _) in stage1))

    for c in range(n_chains):
        lanes = slice(c * LANES, (c + 1) * LANES)
        (v1, r1), (v2, r2) = stage1[c]
        cand, _, zsum = final[c]
        sel = jnp.where(cand == -jnp.inf, 1.0, 0.0)
        na = jnp.zeros(r1.shape, F32)
        for r in range(PEER_TOPK):
            if r == 0:
                n_r = jnp.sum(sel[0:PEER_TOPK, :], axis=0, keepdims=True)
            elif r < PEER_WIDE:
                lo = PEER_TOPK + (r - 1) * PEER_WIDE
                n_r = jnp.sum(sel[lo:lo + PEER_WIDE, :], axis=0, keepdims=True)
            else:
                n_r = sel[head + r - PEER_WIDE:head + r - PEER_WIDE + 1, :]
            na = jnp.where(r1 == float(r), n_r, na)
        na_ref[:, lanes] = na
        r2_ref[:, lanes] = r2.astype(r2_ref.dtype)
        e1z_ref[:, lanes] = jnp.exp(s1_all[:, lanes] - v1[0:1, :]) / zsum
        e2_ref[:, lanes] = jnp.exp(s2_all[:, lanes] - v2[0:1, :]).astype(e2_ref.dtype)


def peer_topk(qry, sub_keys):
    m = qry.shape[0]
    tt = min(PEER_TT, m)
    out_spec = pl.BlockSpec((None, PEER_N_KEYS, tt), lambda i, h: (h, 0, i))
    shape = (PEER_HEADS, PEER_N_KEYS, m)
    return pl.pallas_call(
        _peer_topk_kernel,
        grid=(m // tt, PEER_HEADS),
        in_specs=[pl.BlockSpec((tt, 2 * PEER_N_KEYS), lambda i, h: (i, h)),
                  pl.BlockSpec((None, 2, PEER_N_KEYS, PEER_N_KEYS), lambda i, h: (h, 0, 0, 0))],
        out_specs=[out_spec, out_spec, out_spec, out_spec],
        out_shape=[jax.ShapeDtypeStruct(shape, F32), jax.ShapeDtypeStruct(shape, F32),
                   jax.ShapeDtypeStruct(shape, BF16), jax.ShapeDtypeStruct(shape, BF16)],
        compiler_params=_params("parallel", "parallel"),
        name="peer_topk",
    )(qry, sub_keys)


PEER_TE = 512
PEER_SUB = PEER_TE // PEER_N_KEYS


PEER_ROWS = 16

_ERF_T = 0.3275911
_ERF_C = (0.254829592, -0.284496736, 1.421413741, -1.453152027, 1.061405429)


def _gelu(x):
    z = x * (2.0 ** -0.5)
    t = 1.0 / (1.0 + _ERF_T * jnp.abs(z))
    half = 0.5 * _ERF_C[4]
    for c in _ERF_C[3::-1]:
        half = half * t + 0.5 * c
    half = half * t * jnp.exp(-(z * z))
    return x * jnp.where(z >= 0, 1.0 - half, half)


def _peer_mix_kernel(xt_ref, h_ref, gf_ref, u_ref, v_ref, na_ref, e1z_ref, r2_ref, e2_ref, o_ref, w_ref):
    e = pl.program_id(1)
    n_tiles = pl.num_programs(1) - 1
    tm, d = o_ref.shape
    groups = PEER_N_KEYS // PEER_ROWS

    @pl.when(e == 0)
    def _():
        o_ref[...] = jnp.zeros(o_ref.shape, F32)
        w_ref[...] = jnp.zeros(w_ref.shape, BF16)

    o_ref[...] += jnp.dot(w_ref[...], v_ref[...], preferred_element_type=F32)

    tile = jnp.minimum(e, n_tiles - 1)
    hid = jnp.dot(u_ref[...], xt_ref[...], preferred_element_type=F32)
    parts = []
    for sub in range(PEER_SUB):
        a = tile * PEER_SUB + sub
        act = _gelu(hid[sub * PEER_N_KEYS:(sub + 1) * PEER_N_KEYS, :]).astype(BF16)
        w = jnp.zeros((groups, PEER_ROWS, tm), BF16)
        for hd in range(PEER_HEADS):
            na = jnp.broadcast_to(na_ref[hd, pl.ds(a, 1), :], (PEER_ROWS, tm)).astype(BF16)
            e1z = jnp.broadcast_to(e1z_ref[hd, pl.ds(a, 1), :], (PEER_ROWS, tm)).astype(BF16)
            r2 = r2_ref[hd].reshape(groups, PEER_ROWS, tm)
            e2 = e2_ref[hd].reshape(groups, PEER_ROWS, tm)
            w = w + jnp.where(r2 < na[None], e2, jnp.zeros_like(e2)) * e1z[None]
        parts.append(w.reshape(PEER_N_KEYS, tm) * act)
    w_ref[...] = jnp.concatenate(parts, axis=0).T

    @pl.when(e == n_tiles)
    def _():
        o_ref[...] = _rms(h_ref[...] + o_ref[...], gf_ref[...])


def peer_mix(xt, h, g_final, u, v, na, e1z, r2, e2, *, tm):
    m, d = h.shape
    tm = min(tm, m)
    once = pl.Buffered(1)
    sel = pl.BlockSpec((PEER_HEADS, PEER_N_KEYS, tm), lambda i, e: (0, 0, i), pipeline_mode=once)
    n_tiles = PEER_N_EXPERTS // PEER_TE
    return pl.pallas_call(
        _peer_mix_kernel,
        grid=(m // tm, n_tiles + 1),
        in_specs=[pl.BlockSpec((d, tm), lambda i, e: (0, i), pipeline_mode=once),
                  pl.BlockSpec((tm, d), lambda i, e: (i, 0), pipeline_mode=once),
                  pl.BlockSpec((1, d), lambda i, e: (0, 0)),
                  pl.BlockSpec((PEER_TE, d), lambda i, e: (jnp.minimum(e, n_tiles - 1), 0)),
                  pl.BlockSpec((PEER_TE, d), lambda i, e: (jnp.maximum(e - 1, 0), 0)),
                  sel, sel, sel, sel],
        out_specs=pl.BlockSpec((tm, d), lambda i, e: (i, 0), pipeline_mode=once),
        out_shape=jax.ShapeDtypeStruct((m, d), F32),
        scratch_shapes=[pltpu.VMEM((tm, PEER_TE), BF16)],
        compiler_params=_params("parallel", "arbitrary"),
        name="peer_mix",
    )(xt, h, g_final.reshape(1, d), u, v, na, e1z, r2, e2)


def _heads(x, n):
    return x.reshape(x.shape[0], x.shape[1], n, HEAD_DIM)


def kernel(x_prompt, x_sample, cache_win_k, cache_win_v, state_hgrn, cache_mem_k, cache_mem_v, mem_prompt, norm_mix, w_in, lb_table, g_att_out, g_rec_out, w_out, norm_mem_x, norm_mem_kv, w_mem_q, w_mem_k, w_mem_v, w_mem_o, norm_ffn, peer_w_query, peer_sub_keys, peer_u, peer_v, norm_final):
    bp, sp, d = x_prompt.shape
    bs, ts, _ = x_sample.shape
    depth = w_in.shape[0]
    assert depth == 1
    rope_p = _rope_tables(jnp.arange(sp, dtype=jnp.int32))
    rope_s = _rope_tables(jnp.tile(PAST_LEN + jnp.arange(ts, dtype=jnp.int32), bs))
    lower_bounds = jnp.cumsum(jax.nn.softmax(lb_table.astype(F32), axis=0), axis=0)
    hp = x_prompt.reshape(bp * sp, d)
    hs = x_sample.reshape(bs * ts, d)
    l = 0
    lb = lower_bounds[l]
    w_in_b, w_out_b = w_in[l].astype(BF16), w_out[l].astype(BF16)

    zp = norm_matmul(hp, norm_mix[l], w_in_b, rope=rope_p, rope_cols=2 * ATT_WIDTH).reshape(bp, sp, IN_COLS)
    a_p = attention_prompt(zp, g_att_out[l])
    r_p, st_p = hgrn(zp, lb, g_rec_out[l], jnp.zeros((bp, N_REC_HEADS, HEAD_DIM, HEAD_DIM), F32),
                     chunk=32, valid=32, l_blk=1024, out_dtype=BF16)
    mix_p = jnp.concatenate([a_p, r_p], axis=-1).reshape(bp * sp, d)
    hp = matmul_residual(mix_p, w_out_b, hp)
    wk_p = _heads(zp[:, sp - WIN_BUF:, ATT_WIDTH:2 * ATT_WIDTH], N_ATT_HEADS)
    wv_p = _heads(zp[:, sp - WIN_BUF:, 2 * ATT_WIDTH:3 * ATT_WIDTH], N_ATT_HEADS)

    zs = norm_matmul(hs, norm_mix[l], w_in_b, rope=rope_s, rope_cols=2 * ATT_WIDTH).reshape(bs, ts, IN_COLS)
    a_s, wk_s, wv_s = attention_sample(_heads(zs[..., :ATT_WIDTH], N_ATT_HEADS),
                                       _heads(zs[..., ATT_WIDTH:2 * ATT_WIDTH], N_ATT_HEADS),
                                       _heads(zs[..., 2 * ATT_WIDTH:3 * ATT_WIDTH], N_ATT_HEADS),
                                       cache_win_k[l], cache_win_v[l], g_att_out[l])
    pad_t = 16
    zs_pad = jnp.pad(zs, ((0, 0), (0, pad_t - ts), (0, 0)))
    r_s, st_s = hgrn(zs_pad, lb, g_rec_out[l], state_hgrn[l], chunk=pad_t, valid=ts, l_blk=pad_t, out_dtype=F32)
    mix_s = jnp.concatenate([a_s, r_s[:, :ts]], axis=-1).reshape(bs * ts, d).astype(BF16)
    hs = matmul_residual(mix_s, w_out_b, hs)

    w_kv = jnp.concatenate([w_mem_k[l], w_mem_v[l]], axis=1).astype(BF16)
    mkv = norm_matmul(mem_prompt.reshape(bp * N_MEM, d), norm_mem_kv[l], w_kv)
    mk_p = mkv[:, :MEM_WIDTH].reshape(bp, N_MEM, MEM_WIDTH)
    mv_p = mkv[:, MEM_WIDTH:].reshape(bp, N_MEM, MEM_WIDTH)
    w_q_b, w_o_b = w_mem_q[l].astype(BF16), w_mem_o[l].astype(BF16)
    qm_p = norm_matmul(hp, norm_mem_x[l], w_q_b).reshape(bp, sp, MEM_WIDTH)
    hp = matmul_residual(mem_attention(qm_p, mk_p, mv_p, tq=512).reshape(bp * sp, MEM_WIDTH), w_o_b, hp)
    qm_s = norm_matmul(hs, norm_mem_x[l], w_q_b).reshape(bs, ts, MEM_WIDTH)
    qm_s = jnp.pad(qm_s, ((0, 0), (0, pad_t - ts), (0, 0)))
    om_s = mem_attention(qm_s, cache_mem_k[l].reshape(bs, N_MEM, MEM_WIDTH), cache_mem_v[l].reshape(bs, N_MEM, MEM_WIDTH), tq=pad_t)
    hs = matmul_residual(om_s[:, :ts].reshape(bs * ts, MEM_WIDTH), w_o_b, hs)

    w_pq = peer_w_query[l].astype(BF16)
    u_b, v_b = peer_u[l].astype(BF16), peer_v[l].astype(BF16)
    qry_p, xn_p = norm_matmul(hp, norm_ffn[l], w_pq, return_normed=True)
    y_p = peer_mix(xn_p.T, hp, norm_final, u_b, v_b, *peer_topk(qry_p, peer_sub_keys[l]), tm=512)
    qry_s, xn_s = norm_matmul(hs, norm_ffn[l], w_pq, return_normed=True)
    y_s = peer_mix(xn_s.T, hs, norm_final, u_b, v_b, *peer_topk(qry_s, peer_sub_keys[l]), tm=128)

    return (y_p.reshape(bp, sp, d), y_s.reshape(bs, ts, d),
            wk_p[None], wv_p[None], st_p[None],
            _heads(mk_p, MEM_HEADS)[None], _heads(mv_p, MEM_HEADS)[None],
            wk_s[None], wv_s[None], st_s[None])
```

```python
import functools

import jax
import jax.numpy as jnp
from jax import lax
from jax.experimental import pallas as pl
from jax.experimental.pallas import tpu as pltpu

F32 = jnp.float32
BF16 = jnp.bfloat16

D_MODEL = 4096
HEAD_DIM = 128
ATT_WIDTH = 2048
REC_WIDTH = 2048
N_ATT_HEADS = 16
N_REC_HEADS = 16
IN_COLS = 3 * ATT_WIDTH + 4 * REC_WIDTH
DILATED_PATTERNS = ((128, 1), (512, 4), (2048, 16))
WIN_BUF = 2048
PAST_LEN = 8192
ATT_SCALE = HEAD_DIM ** -0.5
ROT_HALF = HEAD_DIM // 8
ROPE_THETA = 500000.0
N_MEM = 256
MEM_HEADS = 4
MEM_WIDTH = 512
PEER_N_KEYS = 128
PEER_N_EXPERTS = PEER_N_KEYS * PEER_N_KEYS
PEER_HEADS = 8
PEER_TOPK = 16
EPS = 1e-6
NEG = -1e30

LANES = 128
VMEM_LIMIT = 56 * 1024 * 1024

NT_DIMS = (((1,), (1,)), ((), ()))
TN_DIMS = (((0,), (0,)), ((), ()))


def _params(*sem):
    return pltpu.CompilerParams(dimension_semantics=sem, vmem_limit_bytes=VMEM_LIMIT)


def _rms(x, g):
    return x * lax.rsqrt(jnp.mean(x * x, axis=-1, keepdims=True) + EPS) * g


def _norm_matmul_kernel(*refs, rope_tiles, heads_per_tile, emit_transposed):
    refs = list(refs)
    xn_ref = refs.pop()
    xt_ref = refs.pop() if emit_transposed else None
    if rope_tiles:
        x_ref, g_ref, w_ref, cos_ref, sa_ref, sb_ref, o_ref = refs
    else:
        x_ref, g_ref, w_ref, o_ref = refs
    j = pl.program_id(1)

    @pl.when(j == 0)
    def _():
        xn_ref[...] = _rms(x_ref[...], g_ref[...]).astype(BF16)
        if emit_transposed:
            xt_ref[...] = xn_ref[...].T

    o_ref[...] = jnp.dot(xn_ref[...], w_ref[...], preferred_element_type=F32).astype(o_ref.dtype)
    if rope_tiles:
        @pl.when(j < rope_tiles)
        def _():
            cos, sa, sb = cos_ref[...], sa_ref[...], sb_ref[...]
            for h in range(heads_per_tile):
                zh = o_ref[:, h * LANES:(h + 1) * LANES]
                o_ref[:, h * LANES:(h + 1) * LANES] = (
                    zh * cos + pltpu.roll(zh, LANES - ROT_HALF, 1) * sa + pltpu.roll(zh, ROT_HALF, 1) * sb)


def norm_matmul(x, g, w, *, rope=None, rope_cols=0, out_dtype=F32, tm=512, tn=512, return_normed=False):
    m, k = x.shape
    n = w.shape[1]
    tm, tn = min(tm, m), min(tn, n)
    assert m % tm == 0 and n % tn == 0 and rope_cols % tn == 0
    in_specs = [pl.BlockSpec((tm, k), lambda i, j: (i, 0)),
                pl.BlockSpec((1, k), lambda i, j: (0, 0)),
                pl.BlockSpec((k, tn), lambda i, j: (0, j))]
    args = [x, g.reshape(1, k), w]
    rope_tiles = 0
    if rope is not None:
        cos, sa, sb = rope
        period = cos.shape[0] // tm
        rope_tiles = rope_cols // tn
        spec = pl.BlockSpec((tm, LANES), lambda i, j: (i % period, 0))
        in_specs += [spec, spec, spec]
        args += [cos, sa, sb]
    out_specs = [pl.BlockSpec((tm, tn), lambda i, j: (i, j))]
    out_shape = [jax.ShapeDtypeStruct((m, n), out_dtype)]
    if return_normed:
        out_specs.append(pl.BlockSpec((k, tm), lambda i, j: (0, i)))
        out_shape.append(jax.ShapeDtypeStruct((k, m), BF16))
    out = pl.pallas_call(
        functools.partial(_norm_matmul_kernel, rope_tiles=rope_tiles, heads_per_tile=tn // LANES,
                          emit_transposed=return_normed),
        grid=(m // tm, n // tn),
        in_specs=in_specs,
        out_specs=out_specs,
        out_shape=out_shape,
        scratch_shapes=[pltpu.VMEM((tm, k), BF16)],
        compiler_params=_params("parallel", "arbitrary"),
        name="norm_matmul",
    )(*args)
    return out if return_normed else out[0]


def _rope_tables(pos):
    inv = ROPE_THETA ** (-jnp.arange(ROT_HALF, dtype=F32) / ROT_HALF)
    ang = pos.astype(F32)[:, None] * inv[None, :]
    cos, sin = jnp.cos(ang), jnp.sin(ang)
    t = pos.shape[0]
    rest = LANES - 2 * ROT_HALF
    cos_t = jnp.concatenate([cos, cos, jnp.ones((t, rest), F32)], axis=1)
    sa = jnp.concatenate([-sin, jnp.zeros((t, LANES - ROT_HALF), F32)], axis=1)
    sb = jnp.concatenate([jnp.zeros((t, ROT_HALF), F32), sin, jnp.zeros((t, rest), F32)], axis=1)
    return cos_t, sa, sb


def _matmul_residual_kernel(a_ref, w_ref, r_ref, o_ref):
    o_ref[...] = r_ref[...] + jnp.dot(a_ref[...], w_ref[...], preferred_element_type=F32)


def matmul_residual(a, w, res, *, tm=512, tn=512):
    m, k = a.shape
    n = w.shape[1]
    tm, tn = min(tm, m), min(tn, n)
    assert m % tm == 0 and n % tn == 0
    return pl.pallas_call(
        _matmul_residual_kernel,
        grid=(m // tm, n // tn),
        in_specs=[pl.BlockSpec((tm, k), lambda i, j: (i, 0)),
                  pl.BlockSpec((k, tn), lambda i, j: (0, j)),
                  pl.BlockSpec((tm, tn), lambda i, j: (i, j))],
        out_specs=pl.BlockSpec((tm, tn), lambda i, j: (i, j)),
        out_shape=jax.ShapeDtypeStruct((m, n), F32),
        compiler_params=_params("parallel", "parallel"),
        name="matmul_residual",
    )(a, w, res)


ATT_TQ = 256
ATT_BACK = WIN_BUF // ATT_TQ
ATT_WIN = WIN_BUF + ATT_TQ


def _distance_multiplicity(delta):
    c = jnp.zeros(delta.shape, F32)
    for window, dil in DILATED_PATTERNS:
        c = c + ((delta >= 0) & (delta <= window) & (delta % dil == 0)).astype(F32)
    return c


def _log_multiplicity(delta):
    c = _distance_multiplicity(delta)
    return jnp.where(c > 0, jnp.log(jnp.maximum(c, 1.0)), NEG)


def _attn_prompt_kernel(q_ref, k_ref, v_ref, g_ref, bias_ref, o_ref, kb_ref, vb_ref, s_ref):
    i = pl.program_id(2)

    @pl.when(i == 0)
    def _():
        kb_ref[0:WIN_BUF] = jnp.zeros((WIN_BUF, HEAD_DIM), BF16)
        vb_ref[0:WIN_BUF] = jnp.zeros((WIN_BUF, HEAD_DIM), BF16)
        kb_ref[WIN_BUF:] = k_ref[...].astype(BF16)
        vb_ref[WIN_BUF:] = v_ref[...].astype(BF16)

    q = (q_ref[...] * ATT_SCALE).astype(BF16)
    win = pl.ds(pl.multiple_of(i * ATT_TQ, ATT_TQ), ATT_WIN)
    s_ref[...] = lax.dot_general(q, kb_ref[win, :], NT_DIMS, preferred_element_type=F32) + bias_ref[...]

    @pl.when(i < ATT_BACK)
    def _():
        col = lax.broadcasted_iota(jnp.int32, (1, ATT_WIN), 1)
        s_ref[...] = jnp.where(col >= WIN_BUF - i * ATT_TQ, s_ref[...], NEG)

    s = s_ref[...]
    p = jnp.exp(s - jnp.max(s, axis=-1, keepdims=True))
    l = jnp.sum(p, axis=-1, keepdims=True)
    o = jnp.dot(p.astype(BF16), vb_ref[win, :], preferred_element_type=F32) / l
    o_ref[...] = _rms(o, g_ref[...]).astype(o_ref.dtype)


def attention_prompt(z, g_att):
    b, s, _ = z.shape
    rq = jnp.arange(ATT_TQ)[:, None]
    col = jnp.arange(ATT_WIN)[None, :]
    bias = _log_multiplicity(WIN_BUF + rq - col)
    kcol, vcol = ATT_WIDTH // HEAD_DIM, 2 * ATT_WIDTH // HEAD_DIM
    return pl.pallas_call(
        _attn_prompt_kernel,
        grid=(b, N_ATT_HEADS, s // ATT_TQ),
        in_specs=[pl.BlockSpec((None, ATT_TQ, HEAD_DIM), lambda bi, h, i: (bi, i, h)),
                  pl.BlockSpec((None, s, HEAD_DIM), lambda bi, h, i: (bi, 0, kcol + h)),
                  pl.BlockSpec((None, s, HEAD_DIM), lambda bi, h, i: (bi, 0, vcol + h)),
                  pl.BlockSpec((1, HEAD_DIM), lambda bi, h, i: (0, h)),
                  pl.BlockSpec((ATT_TQ, ATT_WIN), lambda bi, h, i: (0, 0))],
        out_specs=pl.BlockSpec((None, ATT_TQ, HEAD_DIM), lambda bi, h, i: (bi, i, h)),
        out_shape=jax.ShapeDtypeStruct((b, s, ATT_WIDTH), BF16),
        scratch_shapes=[pltpu.VMEM((WIN_BUF + s, HEAD_DIM), BF16), pltpu.VMEM((WIN_BUF + s, HEAD_DIM), BF16),
                        pltpu.VMEM((ATT_TQ, ATT_WIN), F32)],
        compiler_params=_params("parallel", "parallel", "arbitrary"),
        name="attention_prompt",
    )(z, z, z, g_att.reshape(1, ATT_WIDTH), bias)


SMP_TB = 512
SMP_ROWS = SMP_TB * N_ATT_HEADS


def _attn_sample_kernel(q_ref, kc_ref, vc_ref, kx_ref, vx_ref, kn_ref, vn_ref, bias_ref, biasn_ref, g_ref,
                        wk_ref, wv_ref, o_ref, m_ref, l_ref, acc_ref, *, t_new):
    kblk = pl.program_id(1)
    last = pl.num_programs(1) - 1

    @pl.when(kblk == 0)
    def _():
        m_ref[...] = jnp.full(m_ref.shape, NEG, F32)
        l_ref[...] = jnp.zeros(l_ref.shape, F32)
        acc_ref[...] = jnp.zeros(acc_ref.shape, F32)

    wk_ref[0:SMP_TB - t_new] = kc_ref[t_new:SMP_TB]
    wv_ref[0:SMP_TB - t_new] = vc_ref[t_new:SMP_TB]

    @pl.when(kblk < last)
    def _():
        wk_ref[SMP_TB - t_new:SMP_TB] = kx_ref[...]
        wv_ref[SMP_TB - t_new:SMP_TB] = vx_ref[...]

    @pl.when(kblk == last)
    def _():
        wk_ref[SMP_TB - t_new:SMP_TB] = kn_ref[...]
        wv_ref[SMP_TB - t_new:SMP_TB] = vn_ref[...]

    q = q_ref[...]

    def update(k2, v2, bias):
        s = lax.dot_general(q, k2.astype(BF16), NT_DIMS, preferred_element_type=F32) + bias
        m_old = m_ref[...]
        m_new = jnp.maximum(m_old, jnp.max(s, axis=-1, keepdims=True))
        alpha = jnp.exp(m_old - m_new)
        p = jnp.exp(s - m_new)
        l_ref[...] = alpha * l_ref[...] + jnp.sum(p, axis=-1, keepdims=True)
        acc_ref[...] = alpha * acc_ref[...] + jnp.dot(p.astype(BF16), v2.astype(BF16), preferred_element_type=F32)
        m_ref[...] = m_new

    update(kc_ref[...].reshape(SMP_ROWS, HEAD_DIM), vc_ref[...].reshape(SMP_ROWS, HEAD_DIM), bias_ref[...])

    @pl.when(kblk == last)
    def _():
        update(kn_ref[...].reshape(t_new * N_ATT_HEADS, HEAD_DIM), vn_ref[...].reshape(t_new * N_ATT_HEADS, HEAD_DIM),
               biasn_ref[...])
        o_ref[...] = _rms(acc_ref[...] / l_ref[...], g_ref[...])


def attention_sample(q, k_new, v_new, cache_k, cache_v, g_att):
    bd, t_new, nh, dh = q.shape
    nblk = WIN_BUF // SMP_TB
    qt = (q * ATT_SCALE).transpose(0, 2, 1, 3).reshape(bd, nh * t_new, dh).astype(BF16)
    hq = jnp.arange(nh * t_new)[:, None] // t_new
    jq = jnp.arange(nh * t_new)[:, None] % t_new
    def make_bias(tok, hk):
        return jnp.where(hq == hk, _log_multiplicity(WIN_BUF + jq - tok), NEG)
    cols = jnp.arange(WIN_BUF * nh)[None, :]
    bias = make_bias(cols // nh, cols % nh).reshape(nh * t_new, nblk, SMP_ROWS).transpose(1, 0, 2)
    cols_n = jnp.arange(t_new * nh)[None, :]
    bias_n = make_bias(WIN_BUF + cols_n // nh, cols_n % nh)
    g_rows = jnp.repeat(g_att.reshape(nh, dh), t_new, axis=0)
    sub = SMP_TB // t_new
    cache_blk = pl.BlockSpec((None, SMP_TB, nh, dh), lambda b, k: (b, k, 0, 0))
    next_blk = pl.BlockSpec((None, t_new, nh, dh), lambda b, k: (b, jnp.minimum((k + 1) * sub, WIN_BUF // t_new - 1), 0, 0))
    new_blk = pl.BlockSpec((None, t_new, nh, dh), lambda b, k: (b, 0, 0, 0))
    wk, wv, o = pl.pallas_call(
        functools.partial(_attn_sample_kernel, t_new=t_new),
        grid=(bd, nblk),
        in_specs=[pl.BlockSpec((None, nh * t_new, dh), lambda b, k: (b, 0, 0)),
                  cache_blk, cache_blk, next_blk, next_blk, new_blk, new_blk,
                  pl.BlockSpec((None, nh * t_new, SMP_ROWS), lambda b, k: (k, 0, 0)),
                  pl.BlockSpec((nh * t_new, nh * t_new), lambda b, k: (0, 0)),
                  pl.BlockSpec((nh * t_new, dh), lambda b, k: (0, 0))],
        out_specs=[cache_blk, cache_blk, pl.BlockSpec((None, nh * t_new, dh), lambda b, k: (b, 0, 0))],
        out_shape=[jax.ShapeDtypeStruct(cache_k.shape, F32), jax.ShapeDtypeStruct(cache_v.shape, F32),
                   jax.ShapeDtypeStruct((bd, nh * t_new, dh), F32)],
        scratch_shapes=[pltpu.VMEM((nh * t_new, 1), F32), pltpu.VMEM((nh * t_new, 1), F32),
                        pltpu.VMEM((nh * t_new, dh), F32)],
        compiler_params=_params("parallel", "arbitrary"),
        name="attention_sample",
    )(qt, cache_k, cache_v, cache_k, cache_v, k_new, v_new, bias, bias_n, g_rows)
    a = o.reshape(bd, nh, t_new, dh).transpose(0, 2, 1, 3).reshape(bd, t_new, nh * dh)
    return a, wk, wv


def _sigmoid_pair(x):
    e = jnp.exp(-jnp.abs(x))
    r = 1.0 / (1.0 + e)
    er = e * r
    pos = x >= 0
    return jnp.where(pos, r, er), jnp.where(pos, er, r)


REC_GROUP = 128
SUBLANES = 8


def _hgrn_kernel(q_ref, f_ref, i_ref, gz_ref, lb_ref, g_ref, s0_ref, r_ref, sfin_ref,
                 st_ref, oin_ref, qd_ref, u_ref, dec_ref, *, chunk, valid):
    lblk = pl.program_id(2)
    l_blk = q_ref.shape[0]
    grp = min(REC_GROUP, l_blk)
    per = grp // chunk

    @pl.when(lblk == 0)
    def _():
        for h in range(st_ref.shape[0]):
            st_ref[h] = s0_ref[h].T

    row = lax.broadcasted_iota(jnp.int32, (grp, grp), 0)
    col = lax.broadcasted_iota(jnp.int32, (grp, grp), 1)
    same_chunk_causal = (row // chunk == col // chunk) & (row >= col)
    pos = lax.broadcasted_iota(jnp.int32, (grp, HEAD_DIM), 0) % chunk

    def group_body(g, carry):
        rows = pl.ds(pl.multiple_of(g * grp, grp), grp)
        for h in range(st_ref.shape[0]):
            cols = slice(h * HEAD_DIM, (h + 1) * HEAD_DIM)
            lb = lb_ref[:, cols]
            sig, sig_neg = _sigmoid_pair(f_ref[rows, cols])
            logf = jnp.log(lb + (1.0 - lb) * sig)
            kr = (1.0 - lb) * sig_neg
            if valid < chunk:
                logf = jnp.where(pos < valid, logf, 0.0)
                kr = jnp.where(pos < valid, kr, 0.0)
            bcum = logf
            d = 1
            while d < chunk:
                bcum = bcum + jnp.where(pos >= d, pltpu.roll(bcum, d, 0), 0.0)
                d *= 2
            last = bcum.reshape(per, chunk, HEAD_DIM)[:, chunk - 1:chunk, :]
            blast = jnp.broadcast_to(last, (per, chunk, HEAD_DIM)).reshape(grp, HEAD_DIM)
            qz = q_ref[rows, cols]
            q = qz * _sigmoid_pair(qz)[0]
            v = i_ref[rows, cols].astype(BF16)
            q_dec = (q * jnp.exp(bcum)).astype(BF16)
            k_inv = (kr * jnp.exp(-bcum)).astype(BF16)
            k_dec = (kr * jnp.exp(blast - bcum)).astype(BF16)
            att = lax.dot_general(q_dec, k_inv, NT_DIMS, preferred_element_type=F32)
            att = jnp.where(same_chunk_causal, att, 0.0).astype(BF16)
            oin_ref[h, rows, :] = jnp.dot(att, v, preferred_element_type=F32)
            qd_ref[h, rows, :] = q_dec
            for c in range(per):
                cr = slice(c * chunk, (c + 1) * chunk)
                u_ref[h, g * per + c] = lax.dot_general(v[cr], k_dec[cr], TN_DIMS, preferred_element_type=F32)
                dec_ref[h, g * per + c] = jnp.broadcast_to(jnp.exp(last[c]), (SUBLANES, HEAD_DIM))
        return carry

    lax.fori_loop(0, l_blk // grp, group_body, 0)

    def chunk_body(c, carry):
        rows = pl.ds(pl.multiple_of(c * chunk, chunk), chunk)
        for h in range(st_ref.shape[0]):
            cols = slice(h * HEAD_DIM, (h + 1) * HEAD_DIM)
            st = st_ref[h]
            o = oin_ref[h, rows, :] + lax.dot_general(qd_ref[h, rows, :], st.astype(BF16), NT_DIMS,
                                                      preferred_element_type=F32)
            st3 = st.reshape(HEAD_DIM // SUBLANES, SUBLANES, HEAD_DIM) * dec_ref[h, c][None]
            st_ref[h] = st3.reshape(HEAD_DIM, HEAD_DIM) + u_ref[h, c]
            gz = gz_ref[rows, cols]
            gate = gz * _sigmoid_pair(gz)[0]
            r_ref[rows, cols] = (_rms(o, g_ref[:, cols]) * gate).astype(r_ref.dtype)
        return carry

    lax.fori_loop(0, l_blk // chunk, chunk_body, 0, unroll=min(4, l_blk // chunk))

    @pl.when(lblk == pl.num_programs(2) - 1)
    def _():
        for h in range(st_ref.shape[0]):
            sfin_ref[h] = st_ref[h].T


def hgrn(z, lb, g_rec, s0, *, chunk, valid, l_blk, heads, out_dtype):
    b, l, _ = z.shape
    assert l % l_blk == 0 and l_blk % chunk == 0 and N_REC_HEADS % heads == 0
    width = heads * HEAD_DIM
    base = 3 * ATT_WIDTH // width
    per = REC_WIDTH // width
    zspec = lambda k: pl.BlockSpec((None, l_blk, width), lambda bi, hg, li: (bi, li, base + k * per + hg))
    vec = pl.BlockSpec((1, width), lambda bi, hg, li: (0, hg))
    st = pl.BlockSpec((None, heads, HEAD_DIM, HEAD_DIM), lambda bi, hg, li: (bi, hg, 0, 0))
    return pl.pallas_call(
        functools.partial(_hgrn_kernel, chunk=chunk, valid=valid),
        grid=(b, N_REC_HEADS // heads, l // l_blk),
        in_specs=[zspec(0), zspec(1), zspec(2), zspec(3), vec, vec, st],
        out_specs=[pl.BlockSpec((None, l_blk, width), lambda bi, hg, li: (bi, li, hg)), st],
        out_shape=[jax.ShapeDtypeStruct((b, l, REC_WIDTH), out_dtype),
                   jax.ShapeDtypeStruct((b, N_REC_HEADS, HEAD_DIM, HEAD_DIM), F32)],
        scratch_shapes=[pltpu.VMEM((heads, HEAD_DIM, HEAD_DIM), F32),
                        pltpu.VMEM((heads, l_blk, HEAD_DIM), F32),
                        pltpu.VMEM((heads, l_blk, HEAD_DIM), BF16),
                        pltpu.VMEM((heads, l_blk // chunk, HEAD_DIM, HEAD_DIM), F32),
                        pltpu.VMEM((heads, l_blk // chunk, SUBLANES, HEAD_DIM), F32)],
        compiler_params=_params("parallel", "parallel", "arbitrary"),
        name="hgrn",
    )(z, z, z, z, lb.reshape(1, REC_WIDTH), g_rec.reshape(1, REC_WIDTH), s0)


def _mem_attn_kernel(q_ref, k_ref, v_ref, o_ref):
    scale = HEAD_DIM ** -0.5
    for h in range(MEM_HEADS):
        cols = slice(h * HEAD_DIM, (h + 1) * HEAD_DIM)
        q = (q_ref[:, cols] * scale).astype(BF16)
        s = lax.dot_general(q, k_ref[:, cols].astype(BF16), NT_DIMS, preferred_element_type=F32)
        p = jnp.exp(s - jnp.max(s, axis=-1, keepdims=True))
        l = jnp.sum(p, axis=-1, keepdims=True)
        o = jnp.dot(p.astype(BF16), v_ref[:, cols].astype(BF16), preferred_element_type=F32)
        o_ref[:, cols] = (o / l).astype(o_ref.dtype)


def mem_attention(q, mk, mv, *, tq):
    b, t, w = q.shape
    tq = min(tq, t)
    return pl.pallas_call(
        _mem_attn_kernel,
        grid=(b, t // tq),
        in_specs=[pl.BlockSpec((None, tq, w), lambda bi, i: (bi, i, 0)),
                  pl.BlockSpec((None, N_MEM, w), lambda bi, i: (bi, 0, 0)),
                  pl.BlockSpec((None, N_MEM, w), lambda bi, i: (bi, 0, 0))],
        out_specs=pl.BlockSpec((None, tq, w), lambda bi, i: (bi, i, 0)),
        out_shape=jax.ShapeDtypeStruct((b, t, w), BF16),
        compiler_params=_params("parallel", "parallel"),
        name="mem_attention",
    )(q, mk, mv)


PEER_TT = 256
PEER_MARK = 1e30
PEER_WIDE = 8


def _first_max(x, order, sentinel):
    m = jnp.max(x, axis=0, keepdims=True)
    return m, jnp.min(jnp.where(x == m, order, sentinel), axis=0, keepdims=True)


def _top16_cols(scores):
    n, t = scores[0].shape
    row = lax.broadcasted_iota(jnp.int32, (n, t), 0)
    row_out = lax.broadcasted_iota(jnp.int32, (PEER_TOPK, t), 0)

    def body(it, carry):
        mark = (lax.convert_element_type(it, F32) + 1.0) * -PEER_MARK
        out = []
        for s, vals in carry:
            m, first = _first_max(s, row, n)
            out.append((jnp.where(row == first, mark, s), jnp.where(row_out == it, m, vals)))
        return tuple(out)

    done = lax.fori_loop(0, PEER_TOPK, body, tuple((s, jnp.zeros((PEER_TOPK, t), F32)) for s in scores))
    return [(vals, jnp.where(s <= -0.5 * PEER_MARK, jnp.floor(s * (-1.0 / PEER_MARK) - 0.5), float(PEER_TOPK)))
            for s, vals in done]


def _candidates(v1, v2):
    parts = [v1[0:1, :] + v2]
    parts += [v1[r:r + 1, :] + v2[0:PEER_WIDE, :] for r in range(1, PEER_WIDE)]
    parts.append(v1[PEER_WIDE:, :] + v2[0:1, :])
    return jnp.concatenate(parts, axis=0)


def _peer_topk_kernel(q_ref, keys_ref, na_ref, e1z_ref, r2_ref, e2_ref):
    s1_all = lax.dot_general(keys_ref[0].astype(BF16), q_ref[:, 0:LANES].astype(BF16), NT_DIMS,
                             preferred_element_type=F32)
    s2_all = lax.dot_general(keys_ref[1].astype(BF16), q_ref[:, LANES:2 * LANES].astype(BF16), NT_DIMS,
                             preferred_element_type=F32)
    n_chains = q_ref.shape[0] // LANES
    stage1 = []
    for c in range(n_chains):
        lanes = slice(c * LANES, (c + 1) * LANES)
        stage1.append(_top16_cols((s1_all[:, lanes], s2_all[:, lanes])))

    head = PEER_TOPK + (PEER_WIDE - 1) * PEER_WIDE
    n_cand = head + PEER_TOPK - PEER_WIDE
    j = lax.broadcasted_iota(jnp.int32, (n_cand, LANES), 0)
    k = j - PEER_TOPK
    order = jnp.where(j < PEER_TOPK, j,
                      jnp.where(j < head, (1 + k // PEER_WIDE) * PEER_TOPK + k % PEER_WIDE,
                                (j - head + PEER_WIDE) * PEER_TOPK))

    def final_body(it, carry):
        out = []
        for cand, top0, zsum in carry:
            m, first = _first_max(cand, order, PEER_TOPK * PEER_TOPK)
            top0 = jnp.where(it == 0, m, top0)
            out.append((jnp.where(order == first, -jnp.inf, cand), top0, zsum + jnp.exp(m - top0)))
        return tuple(out)

    zeros = jnp.zeros((1, LANES), F32)
    final = lax.fori_loop(0, PEER_TOPK, final_body,
                          tuple((_candidates(v1, v2), zeros, zeros) for (v1, _), (v2, _) in stage1))

    for c in range(n_chains):
        lanes = slice(c * LANES, (c + 1) * LANES)
        (v1, r1), (v2, r2) = stage1[c]
        cand, _, zsum = final[c]
        sel = jnp.where(cand == -jnp.inf, 1.0, 0.0)
        na = jnp.zeros(r1.shape, F32)
        for r in range(PEER_TOPK):
            if r == 0:
                n_r = jnp.sum(sel[0:PEER_TOPK, :], axis=0, keepdims=True)
            elif r < PEER_WIDE:
                lo = PEER_TOPK + (r - 1) * PEER_WIDE
                n_r = jnp.sum(sel[lo:lo + PEER_WIDE, :], axis=0, keepdims=True)
            else:
                n_r = sel[head + r - PEER_WIDE:head + r - PEER_WIDE + 1, :]
            na = jnp.where(r1 == float(r), n_r, na)
        na_ref[:, lanes] = na
        r2_ref[:, lanes] = r2.astype(r2_ref.dtype)
        e1z_ref[:, lanes] = jnp.exp(s1_all[:, lanes] - v1[0:1, :]) / zsum
        e2_ref[:, lanes] = jnp.exp(s2_all[:, lanes] - v2[0:1, :]).astype(e2_ref.dtype)


def peer_topk(qry, sub_keys):
    m = qry.shape[0]
    tt = min(PEER_TT, m)
    out_spec = pl.BlockSpec((None, PEER_N_KEYS, tt), lambda i, h: (h, 0, i))
    shape = (PEER_HEADS, PEER_N_KEYS, m)
    return pl.pallas_call(
        _peer_topk_kernel,
        grid=(m // tt, PEER_HEADS),
        in_specs=[pl.BlockSpec((tt, 2 * PEER_N_KEYS), lambda i, h: (i, h)),
                  pl.BlockSpec((None, 2, PEER_N_KEYS, PEER_N_KEYS), lambda i, h: (h, 0, 0, 0))],
        out_specs=[out_spec, out_spec, out_spec, out_spec],
        out_shape=[jax.ShapeDtypeStruct(shape, F32), jax.ShapeDtypeStruct(shape, F32),
                   jax.ShapeDtypeStruct(shape, BF16), jax.ShapeDtypeStruct(shape, BF16)],
        compiler_params=_params("parallel", "parallel"),
        name="peer_topk",
    )(qry, sub_keys)


PEER_TE = 512
PEER_SUB = PEER_TE // PEER_N_KEYS


PEER_ROWS = 16

_ERF_T = 0.3275911
_ERF_C = (0.254829592, -0.284496736, 1.421413741, -1.453152027, 1.061405429)


def _gelu(x):
    z = x * (2.0 ** -0.5)
    t = 1.0 / (1.0 + _ERF_T * jnp.abs(z))
    half = 0.5 * _ERF_C[4]
    for c in _ERF_C[3::-1]:
        half = half * t + 0.5 * c
    half = half * t * jnp.exp(-(z * z))
    return x * jnp.where(z >= 0, 1.0 - half, half)


def _peer_mix_kernel(xt_ref, h_ref, gf_ref, u_ref, v_ref, na_ref, e1z_ref, r2_ref, e2_ref, o_ref, w_ref):
    e = pl.program_id(1)
    n_tiles = pl.num_programs(1) - 1
    tm, d = o_ref.shape
    groups = PEER_N_KEYS // PEER_ROWS

    @pl.when(e == 0)
    def _():
        o_ref[...] = jnp.zeros(o_ref.shape, F32)
        w_ref[...] = jnp.zeros(w_ref.shape, BF16)

    w_prev = w_ref[...]

    tile = jnp.minimum(e, n_tiles - 1)
    parts = []
    for sub in range(PEER_SUB):
        rows = slice(sub * PEER_N_KEYS, (sub + 1) * PEER_N_KEYS)
        a = tile * PEER_SUB + sub
        hid = jnp.dot(u_ref[rows, :], xt_ref[...], preferred_element_type=F32)
        act = _gelu(hid).astype(BF16)
        w = jnp.zeros((groups, PEER_ROWS, tm), BF16)
        for hd in range(PEER_HEADS):
            na = jnp.broadcast_to(na_ref[hd, pl.ds(a, 1), :], (PEER_ROWS, tm)).astype(BF16)
            e1z = jnp.broadcast_to(e1z_ref[hd, pl.ds(a, 1), :], (PEER_ROWS, tm)).astype(BF16)
            r2 = r2_ref[hd].reshape(groups, PEER_ROWS, tm)
            e2 = e2_ref[hd].reshape(groups, PEER_ROWS, tm)
            w = w + jnp.where(r2 < na[None], e2, jnp.zeros_like(e2)) * e1z[None]
        parts.append(w.reshape(PEER_N_KEYS, tm) * act)
    w_ref[...] = jnp.concatenate(parts, axis=0).T

    o_ref[...] += jnp.dot(w_prev, v_ref[...], preferred_element_type=F32)

    @pl.when(e == n_tiles)
    def _():
        o_ref[...] = _rms(h_ref[...] + o_ref[...], gf_ref[...])


def peer_mix(xt, h, g_final, u, v, na, e1z, r2, e2, *, tm):
    m, d = h.shape
    tm = min(tm, m)
    once = pl.Buffered(1)
    sel = pl.BlockSpec((PEER_HEADS, PEER_N_KEYS, tm), lambda i, e: (0, 0, i), pipeline_mode=once)
    n_tiles = PEER_N_EXPERTS // PEER_TE
    return pl.pallas_call(
        _peer_mix_kernel,
        grid=(m // tm, n_tiles + 1),
        in_specs=[pl.BlockSpec((d, tm), lambda i, e: (0, i), pipeline_mode=once),
                  pl.BlockSpec((tm, d), lambda i, e: (i, 0), pipeline_mode=once),
                  pl.BlockSpec((1, d), lambda i, e: (0, 0)),
                  pl.BlockSpec((PEER_TE, d), lambda i, e: (jnp.minimum(e, n_tiles - 1), 0)),
                  pl.BlockSpec((PEER_TE, d), lambda i, e: (jnp.maximum(e - 1, 0), 0)),
                  sel, sel, sel, sel],
        out_specs=pl.BlockSpec((tm, d), lambda i, e: (i, 0), pipeline_mode=once),
        out_shape=jax.ShapeDtypeStruct((m, d), F32),
        scratch_shapes=[pltpu.VMEM((tm, PEER_TE), BF16)],
        compiler_params=_params("parallel", "arbitrary"),
        name="peer_mix",
    )(xt, h, g_final.reshape(1, d), u, v, na, e1z, r2, e2)


def _heads(x, n):
    return x.reshape(x.shape[0], x.shape[1], n, HEAD_DIM)


def kernel(x_prompt, x_sample, cache_win_k, cache_win_v, state_hgrn, cache_mem_k, cache_mem_v, mem_prompt, norm_mix, w_in, lb_table, g_att_out, g_rec_out, w_out, norm_mem_x, norm_mem_kv, w_mem_q, w_mem_k, w_mem_v, w_mem_o, norm_ffn, peer_w_query, peer_sub_keys, peer_u, peer_v, norm_final):
    bp, sp, d = x_prompt.shape
    bs, ts, _ = x_sample.shape
    depth = w_in.shape[0]
    assert depth == 1
    rope_p = _rope_tables(jnp.arange(sp, dtype=jnp.int32))
    rope_s = _rope_tables(jnp.tile(PAST_LEN + jnp.arange(ts, dtype=jnp.int32), bs))
    lower_bounds = jnp.cumsum(jax.nn.softmax(lb_table.astype(F32), axis=0), axis=0)
    hp = x_prompt.reshape(bp * sp, d)
    hs = x_sample.reshape(bs * ts, d)
    l = 0
    lb = lower_bounds[l]
    w_in_b, w_out_b = w_in[l].astype(BF16), w_out[l].astype(BF16)

    zp = norm_matmul(hp, norm_mix[l], w_in_b, rope=rope_p, rope_cols=2 * ATT_WIDTH, tn=1024).reshape(bp, sp, IN_COLS)
    a_p = attention_prompt(zp, g_att_out[l])
    r_p, st_p = hgrn(zp, lb, g_rec_out[l], jnp.zeros((bp, N_REC_HEADS, HEAD_DIM, HEAD_DIM), F32),
                     chunk=32, valid=32, l_blk=1024, heads=4, out_dtype=BF16)
    mix_p = jnp.concatenate([a_p, r_p], axis=-1).reshape(bp * sp, d)
    hp = matmul_residual(mix_p, w_out_b, hp)
    wk_p = _heads(zp[:, sp - WIN_BUF:, ATT_WIDTH:2 * ATT_WIDTH], N_ATT_HEADS)
    wv_p = _heads(zp[:, sp - WIN_BUF:, 2 * ATT_WIDTH:3 * ATT_WIDTH], N_ATT_HEADS)

    zs = norm_matmul(hs, norm_mix[l], w_in_b, rope=rope_s, rope_cols=2 * ATT_WIDTH).reshape(bs, ts, IN_COLS)
    a_s, wk_s, wv_s = attention_sample(_heads(zs[..., :ATT_WIDTH], N_ATT_HEADS),
                                       _heads(zs[..., ATT_WIDTH:2 * ATT_WIDTH], N_ATT_HEADS),
                                       _heads(zs[..., 2 * ATT_WIDTH:3 * ATT_WIDTH], N_ATT_HEADS),
                                       cache_win_k[l], cache_win_v[l], g_att_out[l])
    pad_t = 16
    zs_pad = jnp.pad(zs, ((0, 0), (0, pad_t - ts), (0, 0)))
    r_s, st_s = hgrn(zs_pad, lb, g_rec_out[l], state_hgrn[l], chunk=pad_t, valid=ts, l_blk=pad_t,
                     heads=N_REC_HEADS, out_dtype=F32)
    mix_s = jnp.concatenate([a_s, r_s[:, :ts]], axis=-1).reshape(bs * ts, d).astype(BF16)
    hs = matmul_residual(mix_s, w_out_b, hs)

    w_kv = jnp.concatenate([w_mem_k[l], w_mem_v[l]], axis=1).astype(BF16)
    mkv = norm_matmul(mem_prompt.reshape(bp * N_MEM, d), norm_mem_kv[l], w_kv)
    mk_p = mkv[:, :MEM_WIDTH].reshape(bp, N_MEM, MEM_WIDTH)
    mv_p = mkv[:, MEM_WIDTH:].reshape(bp, N_MEM, MEM_WIDTH)
    w_q_b, w_o_b = w_mem_q[l].astype(BF16), w_mem_o[l].astype(BF16)
    qm_p = norm_matmul(hp, norm_mem_x[l], w_q_b).reshape(bp, sp, MEM_WIDTH)
    hp = matmul_residual(mem_attention(qm_p, mk_p, mv_p, tq=512).reshape(bp * sp, MEM_WIDTH), w_o_b, hp, tn=d)
    qm_s = norm_matmul(hs, norm_mem_x[l], w_q_b).reshape(bs, ts, MEM_WIDTH)
    qm_s = jnp.pad(qm_s, ((0, 0), (0, pad_t - ts), (0, 0)))
    om_s = mem_attention(qm_s, cache_mem_k[l].reshape(bs, N_MEM, MEM_WIDTH), cache_mem_v[l].reshape(bs, N_MEM, MEM_WIDTH), tq=pad_t)
    hs = matmul_residual(om_s[:, :ts].reshape(bs * ts, MEM_WIDTH), w_o_b, hs)

    w_pq = peer_w_query[l].astype(BF16)
    u_b, v_b = peer_u[l].astype(BF16), peer_v[l].astype(BF16)
    qry_p, xt_p = norm_matmul(hp, norm_ffn[l], w_pq, return_normed=True)
    y_p = peer_mix(xt_p, hp, norm_final, u_b, v_b, *peer_topk(qry_p, peer_sub_keys[l]), tm=512)
    qry_s, xt_s = norm_matmul(hs, norm_ffn[l], w_pq, return_normed=True)
    y_s = peer_mix(xt_s, hs, norm_final, u_b, v_b, *peer_topk(qry_s, peer_sub_keys[l]), tm=128)

    return (y_p.reshape(bp, sp, d), y_s.reshape(bs, ts, d),
            wk_p[None], wv_p[None], st_p[None],
            _heads(mk_p, MEM_HEADS)[None], _heads(mv_p, MEM_HEADS)[None],
            wk_s[None], wv_s[None], st_s[None])
```

```python
import functools

import jax
import jax.numpy as jnp
from jax import lax
from jax.experimental import pallas as pl
from jax.experimental.pallas import tpu as pltpu

F32 = jnp.float32
BF16 = jnp.bfloat16

D_MODEL = 4096
HEAD_DIM = 128
ATT_WIDTH = 2048
REC_WIDTH = 2048
N_ATT_HEADS = 16
N_REC_HEADS = 16
IN_COLS = 3 * ATT_WIDTH + 4 * REC_WIDTH
DILATED_PATTERNS = ((128, 1), (512, 4), (2048, 16))
WIN_BUF = 2048
PAST_LEN = 8192
ATT_SCALE = HEAD_DIM ** -0.5
ROT_HALF = HEAD_DIM // 8
ROPE_THETA = 500000.0
N_MEM = 256
MEM_HEADS = 4
MEM_WIDTH = 512
PEER_N_KEYS = 128
PEER_N_EXPERTS = PEER_N_KEYS * PEER_N_KEYS
PEER_HEADS = 8
PEER_TOPK = 16
EPS = 1e-6
NEG = -1e30

LANES = 128
VMEM_LIMIT = 56 * 1024 * 1024

NT_DIMS = (((1,), (1,)), ((), ()))
TN_DIMS = (((0,), (0,)), ((), ()))


def _params(*sem):
    return pltpu.CompilerParams(dimension_semantics=sem, vmem_limit_bytes=VMEM_LIMIT)


def _rms(x, g):
    return x * lax.rsqrt(jnp.mean(x * x, axis=-1, keepdims=True) + EPS) * g


def _norm_matmul_kernel(*refs, rope_tiles, heads_per_tile, emit_transposed):
    refs = list(refs)
    xn_ref = refs.pop()
    xt_ref = refs.pop() if emit_transposed else None
    if rope_tiles:
        x_ref, g_ref, w_ref, cos_ref, sa_ref, sb_ref, o_ref = refs
    else:
        x_ref, g_ref, w_ref, o_ref = refs
    j = pl.program_id(1)

    @pl.when(j == 0)
    def _():
        xn_ref[...] = _rms(x_ref[...], g_ref[...]).astype(BF16)
        if emit_transposed:
            xt_ref[...] = xn_ref[...].T

    o_ref[...] = jnp.dot(xn_ref[...], w_ref[...], preferred_element_type=F32).astype(o_ref.dtype)
    if rope_tiles:
        @pl.when(j < rope_tiles)
        def _():
            cos, sa, sb = cos_ref[...], sa_ref[...], sb_ref[...]
            for h in range(heads_per_tile):
                zh = o_ref[:, h * LANES:(h + 1) * LANES]
                o_ref[:, h * LANES:(h + 1) * LANES] = (
                    zh * cos + pltpu.roll(zh, LANES - ROT_HALF, 1) * sa + pltpu.roll(zh, ROT_HALF, 1) * sb)


def norm_matmul(x, g, w, *, rope=None, rope_cols=0, out_dtype=F32, tm=512, tn=512, return_normed=False):
    m, k = x.shape
    n = w.shape[1]
    tm, tn = min(tm, m), min(tn, n)
    assert m % tm == 0 and n % tn == 0 and rope_cols % tn == 0
    in_specs = [pl.BlockSpec((tm, k), lambda i, j: (i, 0)),
                pl.BlockSpec((1, k), lambda i, j: (0, 0)),
                pl.BlockSpec((k, tn), lambda i, j: (0, j))]
    args = [x, g.reshape(1, k), w]
    rope_tiles = 0
    if rope is not None:
        cos, sa, sb = rope
        period = cos.shape[0] // tm
        rope_tiles = rope_cols // tn
        spec = pl.BlockSpec((tm, LANES), lambda i, j: (i % period, 0))
        in_specs += [spec, spec, spec]
        args += [cos, sa, sb]
    out_specs = [pl.BlockSpec((tm, tn), lambda i, j: (i, j))]
    out_shape = [jax.ShapeDtypeStruct((m, n), out_dtype)]
    if return_normed:
        out_specs.append(pl.BlockSpec((k, tm), lambda i, j: (0, i)))
        out_shape.append(jax.ShapeDtypeStruct((k, m), BF16))
    out = pl.pallas_call(
        functools.partial(_norm_matmul_kernel, rope_tiles=rope_tiles, heads_per_tile=tn // LANES,
                          emit_transposed=return_normed),
        grid=(m // tm, n // tn),
        in_specs=in_specs,
        out_specs=out_specs,
        out_shape=out_shape,
        scratch_shapes=[pltpu.VMEM((tm, k), BF16)],
        compiler_params=_params("parallel", "arbitrary"),
        name="norm_matmul",
    )(*args)
    return out if return_normed else out[0]


def _rope_tables(pos):
    inv = ROPE_THETA ** (-jnp.arange(ROT_HALF, dtype=F32) / ROT_HALF)
    ang = pos.astype(F32)[:, None] * inv[None, :]
    cos, sin = jnp.cos(ang), jnp.sin(ang)
    t = pos.shape[0]
    rest = LANES - 2 * ROT_HALF
    cos_t = jnp.concatenate([cos, cos, jnp.ones((t, rest), F32)], axis=1)
    sa = jnp.concatenate([-sin, jnp.zeros((t, LANES - ROT_HALF), F32)], axis=1)
    sb = jnp.concatenate([jnp.zeros((t, ROT_HALF), F32), sin, jnp.zeros((t, rest), F32)], axis=1)
    return cos_t, sa, sb


def _matmul_residual_kernel(a_ref, w_ref, r_ref, o_ref):
    o_ref[...] = r_ref[...] + jnp.dot(a_ref[...], w_ref[...], preferred_element_type=F32)


def matmul_residual(a, w, res, *, tm=512, tn=512):
    m, k = a.shape
    n = w.shape[1]
    tm, tn = min(tm, m), min(tn, n)
    assert m % tm == 0 and n % tn == 0
    return pl.pallas_call(
        _matmul_residual_kernel,
        grid=(m // tm, n // tn),
        in_specs=[pl.BlockSpec((tm, k), lambda i, j: (i, 0)),
                  pl.BlockSpec((k, tn), lambda i, j: (0, j)),
                  pl.BlockSpec((tm, tn), lambda i, j: (i, j))],
        out_specs=pl.BlockSpec((tm, tn), lambda i, j: (i, j)),
        out_shape=jax.ShapeDtypeStruct((m, n), F32),
        compiler_params=_params("parallel", "parallel"),
        name="matmul_residual",
    )(a, w, res)


ATT_TQ = 256
ATT_BACK = WIN_BUF // ATT_TQ
ATT_WIN = WIN_BUF + ATT_TQ


def _distance_multiplicity(delta):
    c = jnp.zeros(delta.shape, F32)
    for window, dil in DILATED_PATTERNS:
        c = c + ((delta >= 0) & (delta <= window) & (delta % dil == 0)).astype(F32)
    return c


def _log_multiplicity(delta):
    c = _distance_multiplicity(delta)
    return jnp.where(c > 0, jnp.log(jnp.maximum(c, 1.0)), NEG)


def _attn_prompt_kernel(q_ref, k_ref, v_ref, g_ref, bias_ref, o_ref, kb_ref, vb_ref, s_ref):
    i = pl.program_id(2)

    @pl.when(i == 0)
    def _():
        kb_ref[0:WIN_BUF] = jnp.zeros((WIN_BUF, HEAD_DIM), BF16)
        vb_ref[0:WIN_BUF] = jnp.zeros((WIN_BUF, HEAD_DIM), BF16)
        kb_ref[WIN_BUF:] = k_ref[...].astype(BF16)
        vb_ref[WIN_BUF:] = v_ref[...].astype(BF16)

    q = (q_ref[...] * ATT_SCALE).astype(BF16)
    win = pl.ds(pl.multiple_of(i * ATT_TQ, ATT_TQ), ATT_WIN)
    s_ref[...] = lax.dot_general(q, kb_ref[win, :], NT_DIMS, preferred_element_type=F32) + bias_ref[...]

    @pl.when(i < ATT_BACK)
    def _():
        col = lax.broadcasted_iota(jnp.int32, (1, ATT_WIN), 1)
        s_ref[...] = jnp.where(col >= WIN_BUF - i * ATT_TQ, s_ref[...], NEG)

    s = s_ref[...]
    p = jnp.exp(s - jnp.max(s, axis=-1, keepdims=True))
    l = jnp.sum(p, axis=-1, keepdims=True)
    o = jnp.dot(p.astype(BF16), vb_ref[win, :], preferred_element_type=F32) / l
    o_ref[...] = _rms(o, g_ref[...]).astype(o_ref.dtype)


def attention_prompt(z, g_att):
    b, s, _ = z.shape
    rq = jnp.arange(ATT_TQ)[:, None]
    col = jnp.arange(ATT_WIN)[None, :]
    bias = _log_multiplicity(WIN_BUF + rq - col)
    kcol, vcol = ATT_WIDTH // HEAD_DIM, 2 * ATT_WIDTH // HEAD_DIM
    return pl.pallas_call(
        _attn_prompt_kernel,
        grid=(b, N_ATT_HEADS, s // ATT_TQ),
        in_specs=[pl.BlockSpec((None, ATT_TQ, HEAD_DIM), lambda bi, h, i: (bi, i, h)),
                  pl.BlockSpec((None, s, HEAD_DIM), lambda bi, h, i: (bi, 0, kcol + h)),
                  pl.BlockSpec((None, s, HEAD_DIM), lambda bi, h, i: (bi, 0, vcol + h)),
                  pl.BlockSpec((1, HEAD_DIM), lambda bi, h, i: (0, h)),
                  pl.BlockSpec((ATT_TQ, ATT_WIN), lambda bi, h, i: (0, 0))],
        out_specs=pl.BlockSpec((None, ATT_TQ, HEAD_DIM), lambda bi, h, i: (bi, i, h)),
        out_shape=jax.ShapeDtypeStruct((b, s, ATT_WIDTH), BF16),
        scratch_shapes=[pltpu.VMEM((WIN_BUF + s, HEAD_DIM), BF16), pltpu.VMEM((WIN_BUF + s, HEAD_DIM), BF16),
                        pltpu.VMEM((ATT_TQ, ATT_WIN), F32)],
        compiler_params=_params("parallel", "parallel", "arbitrary"),
        name="attention_prompt",
    )(z, z, z, g_att.reshape(1, ATT_WIDTH), bias)


SMP_TB = 512
SMP_ROWS = SMP_TB * N_ATT_HEADS


def _attn_sample_kernel(q_ref, kc_ref, vc_ref, kn_ref, vn_ref, bias_ref, biasn_ref, g_ref,
                        o_ref, m_ref, l_ref, acc_ref, *, t_new):
    kblk = pl.program_id(1)
    last = pl.num_programs(1) - 1

    @pl.when(kblk == 0)
    def _():
        m_ref[...] = jnp.full(m_ref.shape, NEG, F32)
        l_ref[...] = jnp.zeros(l_ref.shape, F32)
        acc_ref[...] = jnp.zeros(acc_ref.shape, F32)

    q = q_ref[...]

    def update(k2, v2, bias):
        s = lax.dot_general(q, k2.astype(BF16), NT_DIMS, preferred_element_type=F32) + bias
        m_old = m_ref[...]
        m_new = jnp.maximum(m_old, jnp.max(s, axis=-1, keepdims=True))
        alpha = jnp.exp(m_old - m_new)
        p = jnp.exp(s - m_new)
        l_ref[...] = alpha * l_ref[...] + jnp.sum(p, axis=-1, keepdims=True)
        acc_ref[...] = alpha * acc_ref[...] + jnp.dot(p.astype(BF16), v2.astype(BF16), preferred_element_type=F32)
        m_ref[...] = m_new

    update(kc_ref[...].reshape(SMP_ROWS, HEAD_DIM), vc_ref[...].reshape(SMP_ROWS, HEAD_DIM), bias_ref[...])

    @pl.when(kblk == last)
    def _():
        update(kn_ref[...].reshape(t_new * N_ATT_HEADS, HEAD_DIM), vn_ref[...].reshape(t_new * N_ATT_HEADS, HEAD_DIM),
               biasn_ref[...])
        o_ref[...] = _rms(acc_ref[...] / l_ref[...], g_ref[...])


def attention_sample(q, k_new, v_new, cache_k, cache_v, g_att):
    bd, t_new, nh, dh = q.shape
    nblk = WIN_BUF // SMP_TB
    qt = (q * ATT_SCALE).transpose(0, 2, 1, 3).reshape(bd, nh * t_new, dh).astype(BF16)
    hq = jnp.arange(nh * t_new)[:, None] // t_new
    jq = jnp.arange(nh * t_new)[:, None] % t_new
    def make_bias(tok, hk):
        return jnp.where(hq == hk, _log_multiplicity(WIN_BUF + jq - tok), NEG)
    cols = jnp.arange(WIN_BUF * nh)[None, :]
    bias = make_bias(cols // nh, cols % nh).reshape(nh * t_new, nblk, SMP_ROWS).transpose(1, 0, 2)
    cols_n = jnp.arange(t_new * nh)[None, :]
    bias_n = make_bias(WIN_BUF + cols_n // nh, cols_n % nh)
    g_rows = jnp.repeat(g_att.reshape(nh, dh), t_new, axis=0)
    cache_blk = pl.BlockSpec((None, SMP_TB, nh, dh), lambda b, k: (b, k, 0, 0))
    new_blk = pl.BlockSpec((None, t_new, nh, dh), lambda b, k: (b, 0, 0, 0))
    o = pl.pallas_call(
        functools.partial(_attn_sample_kernel, t_new=t_new),
        grid=(bd, nblk),
        in_specs=[pl.BlockSpec((None, nh * t_new, dh), lambda b, k: (b, 0, 0)),
                  cache_blk, cache_blk, new_blk, new_blk,
                  pl.BlockSpec((None, nh * t_new, SMP_ROWS), lambda b, k: (k, 0, 0)),
                  pl.BlockSpec((nh * t_new, nh * t_new), lambda b, k: (0, 0)),
                  pl.BlockSpec((nh * t_new, dh), lambda b, k: (0, 0))],
        out_specs=pl.BlockSpec((None, nh * t_new, dh), lambda b, k: (b, 0, 0)),
        out_shape=jax.ShapeDtypeStruct((bd, nh * t_new, dh), F32),
        scratch_shapes=[pltpu.VMEM((nh * t_new, 1), F32), pltpu.VMEM((nh * t_new, 1), F32),
                        pltpu.VMEM((nh * t_new, dh), F32)],
        compiler_params=_params("parallel", "arbitrary"),
        name="attention_sample",
    )(qt, cache_k, cache_v, k_new, v_new, bias, bias_n, g_rows)
    return o.reshape(bd, nh, t_new, dh).transpose(0, 2, 1, 3).reshape(bd, t_new, nh * dh)


def _sigmoid_pair(x):
    e = jnp.exp(-jnp.abs(x))
    r = 1.0 / (1.0 + e)
    er = e * r
    pos = x >= 0
    return jnp.where(pos, r, er), jnp.where(pos, er, r)


REC_GROUP = 128
SUBLANES = 8


def _hgrn_kernel(q_ref, f_ref, i_ref, gz_ref, lb_ref, g_ref, s0_ref, r_ref, sfin_ref,
                 st_ref, oin_ref, qd_ref, u_ref, dec_ref, *, chunk, valid):
    lblk = pl.program_id(2)
    l_blk = q_ref.shape[0]
    grp = min(REC_GROUP, l_blk)
    per = grp // chunk

    @pl.when(lblk == 0)
    def _():
        for h in range(st_ref.shape[0]):
            st_ref[h] = s0_ref[h].T

    row = lax.broadcasted_iota(jnp.int32, (grp, grp), 0)
    col = lax.broadcasted_iota(jnp.int32, (grp, grp), 1)
    same_chunk_causal = (row // chunk == col // chunk) & (row >= col)
    pos = lax.broadcasted_iota(jnp.int32, (grp, HEAD_DIM), 0) % chunk

    def group_body(g, carry):
        rows = pl.ds(pl.multiple_of(g * grp, grp), grp)
        for h in range(st_ref.shape[0]):
            cols = slice(h * HEAD_DIM, (h + 1) * HEAD_DIM)
            lb = lb_ref[:, cols]
            sig, sig_neg = _sigmoid_pair(f_ref[rows, cols])
            logf = jnp.log(lb + (1.0 - lb) * sig)
            kr = (1.0 - lb) * sig_neg
            if valid < chunk:
                logf = jnp.where(pos < valid, logf, 0.0)
                kr = jnp.where(pos < valid, kr, 0.0)
            bcum = logf
            d = 1
            while d < chunk:
                bcum = bcum + jnp.where(pos >= d, pltpu.roll(bcum, d, 0), 0.0)
                d *= 2
            last = bcum.reshape(per, chunk, HEAD_DIM)[:, chunk - 1:chunk, :]
            blast = jnp.broadcast_to(last, (per, chunk, HEAD_DIM)).reshape(grp, HEAD_DIM)
            qz = q_ref[rows, cols]
            q = qz * _sigmoid_pair(qz)[0]
            v = i_ref[rows, cols].astype(BF16)
            q_dec = (q * jnp.exp(bcum)).astype(BF16)
            k_inv = (kr * jnp.exp(-bcum)).astype(BF16)
            k_dec = (kr * jnp.exp(blast - bcum)).astype(BF16)
            att = lax.dot_general(q_dec, k_inv, NT_DIMS, preferred_element_type=F32)
            att = jnp.where(same_chunk_causal, att, 0.0).astype(BF16)
            oin_ref[h, rows, :] = jnp.dot(att, v, preferred_element_type=F32)
            qd_ref[h, rows, :] = q_dec
            for c in range(per):
                cr = slice(c * chunk, (c + 1) * chunk)
                u_ref[h, g * per + c] = lax.dot_general(v[cr], k_dec[cr], TN_DIMS, preferred_element_type=F32)
                dec_ref[h, g * per + c] = jnp.broadcast_to(jnp.exp(last[c]), (SUBLANES, HEAD_DIM))
        return carry

    lax.fori_loop(0, l_blk // grp, group_body, 0)

    def chunk_body(c, carry):
        rows = pl.ds(pl.multiple_of(c * chunk, chunk), chunk)
        for h in range(st_ref.shape[0]):
            cols = slice(h * HEAD_DIM, (h + 1) * HEAD_DIM)
            st = st_ref[h]
            o = oin_ref[h, rows, :] + lax.dot_general(qd_ref[h, rows, :], st.astype(BF16), NT_DIMS,
                                                      preferred_element_type=F32)
            st3 = st.reshape(HEAD_DIM // SUBLANES, SUBLANES, HEAD_DIM) * dec_ref[h, c][None]
            st_ref[h] = st3.reshape(HEAD_DIM, HEAD_DIM) + u_ref[h, c]
            gz = gz_ref[rows, cols]
            gate = gz * _sigmoid_pair(gz)[0]
            r_ref[rows, cols] = (_rms(o, g_ref[:, cols]) * gate).astype(r_ref.dtype)
        return carry

    lax.fori_loop(0, l_blk // chunk, chunk_body, 0, unroll=min(4, l_blk // chunk))

    @pl.when(lblk == pl.num_programs(2) - 1)
    def _():
        for h in range(st_ref.shape[0]):
            sfin_ref[h] = st_ref[h].T


def hgrn(z, lb, g_rec, s0, *, chunk, valid, l_blk, heads, out_dtype):
    b, l, _ = z.shape
    assert l % l_blk == 0 and l_blk % chunk == 0 and N_REC_HEADS % heads == 0
    width = heads * HEAD_DIM
    base = 3 * ATT_WIDTH // width
    per = REC_WIDTH // width
    zspec = lambda k: pl.BlockSpec((None, l_blk, width), lambda bi, hg, li: (bi, li, base + k * per + hg))
    vec = pl.BlockSpec((1, width), lambda bi, hg, li: (0, hg))
    st = pl.BlockSpec((None, heads, HEAD_DIM, HEAD_DIM), lambda bi, hg, li: (bi, hg, 0, 0))
    return pl.pallas_call(
        functools.partial(_hgrn_kernel, chunk=chunk, valid=valid),
        grid=(b, N_REC_HEADS // heads, l // l_blk),
        in_specs=[zspec(0), zspec(1), zspec(2), zspec(3), vec, vec, st],
        out_specs=[pl.BlockSpec((None, l_blk, width), lambda bi, hg, li: (bi, li, hg)), st],
        out_shape=[jax.ShapeDtypeStruct((b, l, REC_WIDTH), out_dtype),
                   jax.ShapeDtypeStruct((b, N_REC_HEADS, HEAD_DIM, HEAD_DIM), F32)],
        scratch_shapes=[pltpu.VMEM((heads, HEAD_DIM, HEAD_DIM), F32),
                        pltpu.VMEM((heads, l_blk, HEAD_DIM), F32),
                        pltpu.VMEM((heads, l_blk, HEAD_DIM), BF16),
                        pltpu.VMEM((heads, l_blk // chunk, HEAD_DIM, HEAD_DIM), F32),
                        pltpu.VMEM((heads, l_blk // chunk, SUBLANES, HEAD_DIM), F32)],
        compiler_params=_params("parallel", "parallel", "arbitrary"),
        name="hgrn",
    )(z, z, z, z, lb.reshape(1, REC_WIDTH), g_rec.reshape(1, REC_WIDTH), s0)


def _mem_attn_kernel(q_ref, k_ref, v_ref, o_ref):
    scale = HEAD_DIM ** -0.5
    for h in range(MEM_HEADS):
        cols = slice(h * HEAD_DIM, (h + 1) * HEAD_DIM)
        q = (q_ref[:, cols] * scale).astype(BF16)
        s = lax.dot_general(q, k_ref[:, cols].astype(BF16), NT_DIMS, preferred_element_type=F32)
        p = jnp.exp(s - jnp.max(s, axis=-1, keepdims=True))
        l = jnp.sum(p, axis=-1, keepdims=True)
        o = jnp.dot(p.astype(BF16), v_ref[:, cols].astype(BF16), preferred_element_type=F32)
        o_ref[:, cols] = (o / l).astype(o_ref.dtype)


def mem_attention(q, mk, mv, *, tq):
    b, t, w = q.shape
    tq = min(tq, t)
    return pl.pallas_call(
        _mem_attn_kernel,
        grid=(b, t // tq),
        in_specs=[pl.BlockSpec((None, tq, w), lambda bi, i: (bi, i, 0)),
                  pl.BlockSpec((None, N_MEM, w), lambda bi, i: (bi, 0, 0)),
                  pl.BlockSpec((None, N_MEM, w), lambda bi, i: (bi, 0, 0))],
        out_specs=pl.BlockSpec((None, tq, w), lambda bi, i: (bi, i, 0)),
        out_shape=jax.ShapeDtypeStruct((b, t, w), BF16),
        compiler_params=_params("parallel", "parallel"),
        name="mem_attention",
    )(q, mk, mv)


PEER_TT = 256
PEER_MARK = 1e30
PEER_WIDE = 8


def _first_max(x, order, sentinel):
    m = jnp.max(x, axis=0, keepdims=True)
    return m, jnp.min(jnp.where(x == m, order, sentinel), axis=0, keepdims=True)


def _top16_cols(scores):
    n, t = scores[0].shape
    row = lax.broadcasted_iota(jnp.int32, (n, t), 0)
    row_out = lax.broadcasted_iota(jnp.int32, (PEER_TOPK, t), 0)

    def body(it, carry):
        mark = (lax.convert_element_type(it, F32) + 1.0) * -PEER_MARK
        out = []
        for s, vals in carry:
            m, first = _first_max(s, row, n)
            out.append((jnp.where(row == first, mark, s), jnp.where(row_out == it, m, vals)))
        return tuple(out)

    done = lax.fori_loop(0, PEER_TOPK, body, tuple((s, jnp.zeros((PEER_TOPK, t), F32)) for s in scores))
    return [(vals, jnp.where(s <= -0.5 * PEER_MARK, jnp.floor(s * (-1.0 / PEER_MARK) - 0.5), float(PEER_TOPK)))
            for s, vals in done]


def _candidates(v1, v2):
    parts = [v1[0:1, :] + v2]
    parts += [v1[r:r + 1, :] + v2[0:PEER_WIDE, :] for r in range(1, PEER_WIDE)]
    parts.append(v1[PEER_WIDE:, :] + v2[0:1, :])
    return jnp.concatenate(parts, axis=0)


def _peer_topk_kernel(q_ref, keys_ref, na_ref, e1z_ref, r2_ref, e2_ref):
    s1_all = lax.dot_general(keys_ref[0].astype(BF16), q_ref[:, 0:LANES].astype(BF16), NT_DIMS,
                             preferred_element_type=F32)
    s2_all = lax.dot_general(keys_ref[1].astype(BF16), q_ref[:, LANES:2 * LANES].astype(BF16), NT_DIMS,
                             preferred_element_type=F32)
    n_chains = q_ref.shape[0] // LANES
    stage1 = []
    for c in range(n_chains):
        lanes = slice(c * LANES, (c + 1) * LANES)
        stage1.append(_top16_cols((s1_all[:, lanes], s2_all[:, lanes])))

    head = PEER_TOPK + (PEER_WIDE - 1) * PEER_WIDE
    n_cand = head + PEER_TOPK - PEER_WIDE
    j = lax.broadcasted_iota(jnp.int32, (n_cand, LANES), 0)
    k = j - PEER_TOPK
    order = jnp.where(j < PEER_TOPK, j,
                      jnp.where(j < head, (1 + k // PEER_WIDE) * PEER_TOPK + k % PEER_WIDE,
                                (j - head + PEER_WIDE) * PEER_TOPK))

    def final_body(it, carry):
        out = []
        for cand, top0, zsum in carry:
            m, first = _first_max(cand, order, PEER_TOPK * PEER_TOPK)
            top0 = jnp.where(it == 0, m, top0)
            out.append((jnp.where(order == first, -jnp.inf, cand), top0, zsum + jnp.exp(m - top0)))
        return tuple(out)

    zeros = jnp.zeros((1, LANES), F32)
    final = lax.fori_loop(0, PEER_TOPK, final_body,
                          tuple((_candidates(v1, v2), zeros, zeros) for (v1, _), (v2, _) in stage1))

    for c in range(n_chains):
        lanes = slice(c * LANES, (c + 1) * LANES)
        (v1, r1), (v2, r2) = stage1[c]
        cand, _, zsum = final[c]
        sel = jnp.where(cand == -jnp.inf, 1.0, 0.0)
        na = jnp.zeros(r1.shape, F32)
        for r in range(PEER_TOPK):
            if r == 0:
                n_r = jnp.sum(sel[0:PEER_TOPK, :], axis=0, keepdims=True)
            elif r < PEER_WIDE:
                lo = PEER_TOPK + (r - 1) * PEER_WIDE
                n_r = jnp.sum(sel[lo:lo + PEER_WIDE, :], axis=0, keepdims=True)
            else:
                n_r = sel[head + r - PEER_WIDE:head + r - PEER_WIDE + 1, :]
            na = jnp.where(r1 == float(r), n_r, na)
        na_ref[:, lanes] = na
        r2_ref[:, lanes] = r2.astype(r2_ref.dtype)
        e1z_ref[:, lanes] = jnp.exp(s1_all[:, lanes] - v1[0:1, :]) / zsum
        e2_ref[:, lanes] = jnp.exp(s2_all[:, lanes] - v2[0:1, :]).astype(e2_ref.dtype)


def peer_topk(qry, sub_keys):
    m = qry.shape[0]
    tt = min(PEER_TT, m)
    out_spec = pl.BlockSpec((None, PEER_N_KEYS, tt), lambda i, h: (h, 0, i))
    shape = (PEER_HEADS, PEER_N_KEYS, m)
    return pl.pallas_call(
        _peer_topk_kernel,
        grid=(m // tt, PEER_HEADS),
        in_specs=[pl.BlockSpec((tt, 2 * PEER_N_KEYS), lambda i, h: (i, h)),
                  pl.BlockSpec((None, 2, PEER_N_KEYS, PEER_N_KEYS), lambda i, h: (h, 0, 0, 0))],
        out_specs=[out_spec, out_spec, out_spec, out_spec],
        out_shape=[jax.ShapeDtypeStruct(shape, F32), jax.ShapeDtypeStruct(shape, F32),
                   jax.ShapeDtypeStruct(shape, BF16), jax.ShapeDtypeStruct(shape, BF16)],
        compiler_params=_params("parallel", "parallel"),
        name="peer_topk",
    )(qry, sub_keys)


PEER_TE = 512
PEER_SUB = PEER_TE // PEER_N_KEYS


PEER_ROWS = 16

_ERF_T = 0.3275911
_ERF_C = (0.254829592, -0.284496736, 1.421413741, -1.453152027, 1.061405429)


def _gelu(x):
    z = x * (2.0 ** -0.5)
    t = 1.0 / (1.0 + _ERF_T * jnp.abs(z))
    half = 0.5 * _ERF_C[4]
    for c in _ERF_C[3::-1]:
        half = half * t + 0.5 * c
    half = half * t * jnp.exp(-(z * z))
    return x * jnp.where(z >= 0, 1.0 - half, half)


def _window_shift_copies(ck_ref, cv_ref, kn_ref, vn_ref, wk_ref, wv_ref, sem):
    t_new = kn_ref.shape[1]
    keep = WIN_BUF - t_new
    return (pltpu.make_async_copy(ck_ref.at[:, pl.ds(t_new, keep)], wk_ref.at[:, pl.ds(0, keep)], sem.at[0]),
            pltpu.make_async_copy(kn_ref, wk_ref.at[:, pl.ds(keep, t_new)], sem.at[1]),
            pltpu.make_async_copy(cv_ref.at[:, pl.ds(t_new, keep)], wv_ref.at[:, pl.ds(0, keep)], sem.at[2]),
            pltpu.make_async_copy(vn_ref, wv_ref.at[:, pl.ds(keep, t_new)], sem.at[3]))


def _peer_mix_kernel(*refs, carry_window_shift):
    if carry_window_shift:
        (xt_ref, h_ref, gf_ref, u_ref, v_ref, na_ref, e1z_ref, r2_ref, e2_ref, ck_ref, cv_ref, kn_ref, vn_ref,
         o_ref, wk_ref, wv_ref, w_ref, sem) = refs
        copies = _window_shift_copies(ck_ref, cv_ref, kn_ref, vn_ref, wk_ref, wv_ref, sem)
    else:
        xt_ref, h_ref, gf_ref, u_ref, v_ref, na_ref, e1z_ref, r2_ref, e2_ref, o_ref, w_ref = refs
    i = pl.program_id(0)
    e = pl.program_id(1)
    n_tiles = pl.num_programs(1) - 1
    tm, d = o_ref.shape
    groups = PEER_N_KEYS // PEER_ROWS

    if carry_window_shift:
        @pl.when((i == 0) & (e == 0))
        def _():
            for c in copies:
                c.start()

    @pl.when(e == 0)
    def _():
        o_ref[...] = jnp.zeros(o_ref.shape, F32)
        w_ref[...] = jnp.zeros(w_ref.shape, BF16)

    o_ref[...] += jnp.dot(w_ref[...], v_ref[...], preferred_element_type=F32)

    tile = jnp.minimum(e, n_tiles - 1)
    hid = jnp.dot(u_ref[...], xt_ref[...], preferred_element_type=F32)
    parts = []
    for sub in range(PEER_SUB):
        a = tile * PEER_SUB + sub
        act = _gelu(hid[sub * PEER_N_KEYS:(sub + 1) * PEER_N_KEYS, :]).astype(BF16)
        w = jnp.zeros((groups, PEER_ROWS, tm), BF16)
        for hd in range(PEER_HEADS):
            na = jnp.broadcast_to(na_ref[hd, pl.ds(a, 1), :], (PEER_ROWS, tm)).astype(BF16)
            e1z = jnp.broadcast_to(e1z_ref[hd, pl.ds(a, 1), :], (PEER_ROWS, tm)).astype(BF16)
            r2 = r2_ref[hd].reshape(groups, PEER_ROWS, tm)
            e2 = e2_ref[hd].reshape(groups, PEER_ROWS, tm)
            w = w + jnp.where(r2 < na[None], e2, jnp.zeros_like(e2)) * e1z[None]
        parts.append(w.reshape(PEER_N_KEYS, tm) * act)
    w_ref[...] = jnp.concatenate(parts, axis=0).T

    @pl.when(e == n_tiles)
    def _():
        o_ref[...] = _rms(h_ref[...] + o_ref[...], gf_ref[...])

    if carry_window_shift:
        @pl.when((i == pl.num_programs(0) - 1) & (e == n_tiles))
        def _():
            for c in copies:
                c.wait()


def peer_mix(xt, h, g_final, u, v, na, e1z, r2, e2, *, tm, window_shift=None):
    m, d = h.shape
    tm = min(tm, m)
    once = pl.Buffered(1)
    sel = pl.BlockSpec((PEER_HEADS, PEER_N_KEYS, tm), lambda i, e: (0, 0, i), pipeline_mode=once)
    n_tiles = PEER_N_EXPERTS // PEER_TE
    in_specs = [pl.BlockSpec((d, tm), lambda i, e: (0, i), pipeline_mode=once),
                pl.BlockSpec((tm, d), lambda i, e: (i, 0), pipeline_mode=once),
                pl.BlockSpec((1, d), lambda i, e: (0, 0)),
                pl.BlockSpec((PEER_TE, d), lambda i, e: (jnp.minimum(e, n_tiles - 1), 0)),
                pl.BlockSpec((PEER_TE, d), lambda i, e: (jnp.maximum(e - 1, 0), 0)),
                sel, sel, sel, sel]
    args = [xt, h, g_final.reshape(1, d), u, v, na, e1z, r2, e2]
    out_specs = [pl.BlockSpec((tm, d), lambda i, e: (i, 0), pipeline_mode=once)]
    out_shape = [jax.ShapeDtypeStruct((m, d), F32)]
    scratch = [pltpu.VMEM((tm, PEER_TE), BF16)]
    semantics = ("parallel", "arbitrary")
    if window_shift is not None:
        hbm = pl.BlockSpec(memory_space=pl.ANY)
        in_specs += [hbm] * 4
        args += list(window_shift)
        out_specs += [hbm, hbm]
        out_shape += [jax.ShapeDtypeStruct(window_shift[0].shape, F32),
                      jax.ShapeDtypeStruct(window_shift[1].shape, F32)]
        scratch.append(pltpu.SemaphoreType.DMA((4,)))
        semantics = ("arbitrary", "arbitrary")
    out = pl.pallas_call(
        functools.partial(_peer_mix_kernel, carry_window_shift=window_shift is not None),
        grid=(m // tm, n_tiles + 1),
        in_specs=in_specs,
        out_specs=out_specs,
        out_shape=out_shape,
        scratch_shapes=scratch,
        compiler_params=_params(*semantics),
        name="peer_mix",
    )(*args)
    return out if window_shift is not None else out[0]


def _heads(x, n):
    return x.reshape(x.shape[0], x.shape[1], n, HEAD_DIM)


def kernel(x_prompt, x_sample, cache_win_k, cache_win_v, state_hgrn, cache_mem_k, cache_mem_v, mem_prompt, norm_mix, w_in, lb_table, g_att_out, g_rec_out, w_out, norm_mem_x, norm_mem_kv, w_mem_q, w_mem_k, w_mem_v, w_mem_o, norm_ffn, peer_w_query, peer_sub_keys, peer_u, peer_v, norm_final):
    bp, sp, d = x_prompt.shape
    bs, ts, _ = x_sample.shape
    depth = w_in.shape[0]
    assert depth == 1
    rope_p = _rope_tables(jnp.arange(sp, dtype=jnp.int32))
    rope_s = _rope_tables(jnp.tile(PAST_LEN + jnp.arange(ts, dtype=jnp.int32), bs))
    lower_bounds = jnp.cumsum(jax.nn.softmax(lb_table.astype(F32), axis=0), axis=0)
    hp = x_prompt.reshape(bp * sp, d)
    hs = x_sample.reshape(bs * ts, d)
    l = 0
    lb = lower_bounds[l]
    w_in_b, w_out_b = w_in[l].astype(BF16), w_out[l].astype(BF16)

    zp = norm_matmul(hp, norm_mix[l], w_in_b, rope=rope_p, rope_cols=2 * ATT_WIDTH, tn=1024).reshape(bp, sp, IN_COLS)
    a_p = attention_prompt(zp, g_att_out[l])
    r_p, st_p = hgrn(zp, lb, g_rec_out[l], jnp.zeros((bp, N_REC_HEADS, HEAD_DIM, HEAD_DIM), F32),
                     chunk=32, valid=32, l_blk=1024, heads=4, out_dtype=BF16)
    mix_p = jnp.concatenate([a_p, r_p], axis=-1).reshape(bp * sp, d)
    hp = matmul_residual(mix_p, w_out_b, hp)
    wk_p = _heads(zp[:, sp - WIN_BUF:, ATT_WIDTH:2 * ATT_WIDTH], N_ATT_HEADS)
    wv_p = _heads(zp[:, sp - WIN_BUF:, 2 * ATT_WIDTH:3 * ATT_WIDTH], N_ATT_HEADS)

    zs = norm_matmul(hs, norm_mix[l], w_in_b, rope=rope_s, rope_cols=2 * ATT_WIDTH).reshape(bs, ts, IN_COLS)
    k_new = _heads(zs[..., ATT_WIDTH:2 * ATT_WIDTH], N_ATT_HEADS)
    v_new = _heads(zs[..., 2 * ATT_WIDTH:3 * ATT_WIDTH], N_ATT_HEADS)
    a_s = attention_sample(_heads(zs[..., :ATT_WIDTH], N_ATT_HEADS), k_new, v_new,
                           cache_win_k[l], cache_win_v[l], g_att_out[l])
    pad_t = 16
    zs_pad = jnp.pad(zs, ((0, 0), (0, pad_t - ts), (0, 0)))
    r_s, st_s = hgrn(zs_pad, lb, g_rec_out[l], state_hgrn[l], chunk=pad_t, valid=ts, l_blk=pad_t,
                     heads=N_REC_HEADS, out_dtype=F32)
    mix_s = jnp.concatenate([a_s, r_s[:, :ts]], axis=-1).reshape(bs * ts, d).astype(BF16)
    hs = matmul_residual(mix_s, w_out_b, hs)

    w_kv = jnp.concatenate([w_mem_k[l], w_mem_v[l]], axis=1).astype(BF16)
    mkv = norm_matmul(mem_prompt.reshape(bp * N_MEM, d), norm_mem_kv[l], w_kv)
    mk_p = mkv[:, :MEM_WIDTH].reshape(bp, N_MEM, MEM_WIDTH)
    mv_p = mkv[:, MEM_WIDTH:].reshape(bp, N_MEM, MEM_WIDTH)
    w_q_b, w_o_b = w_mem_q[l].astype(BF16), w_mem_o[l].astype(BF16)
    qm_p = norm_matmul(hp, norm_mem_x[l], w_q_b).reshape(bp, sp, MEM_WIDTH)
    hp = matmul_residual(mem_attention(qm_p, mk_p, mv_p, tq=512).reshape(bp * sp, MEM_WIDTH), w_o_b, hp, tn=d)
    qm_s = norm_matmul(hs, norm_mem_x[l], w_q_b).reshape(bs, ts, MEM_WIDTH)
    qm_s = jnp.pad(qm_s, ((0, 0), (0, pad_t - ts), (0, 0)))
    om_s = mem_attention(qm_s, cache_mem_k[l].reshape(bs, N_MEM, MEM_WIDTH), cache_mem_v[l].reshape(bs, N_MEM, MEM_WIDTH), tq=pad_t)
    hs = matmul_residual(om_s[:, :ts].reshape(bs * ts, MEM_WIDTH), w_o_b, hs)

    w_pq = peer_w_query[l].astype(BF16)
    u_b, v_b = peer_u[l].astype(BF16), peer_v[l].astype(BF16)
    qry_p, xt_p = norm_matmul(hp, norm_ffn[l], w_pq, return_normed=True)
    y_p, wk_s, wv_s = peer_mix(xt_p, hp, norm_final, u_b, v_b, *peer_topk(qry_p, peer_sub_keys[l]), tm=512,
                               window_shift=(cache_win_k[l], cache_win_v[l], k_new, v_new))
    qry_s, xt_s = norm_matmul(hs, norm_ffn[l], w_pq, return_normed=True)
    y_s = peer_mix(xt_s, hs, norm_final, u_b, v_b, *peer_topk(qry_s, peer_sub_keys[l]), tm=128)

    return (y_p.reshape(bp, sp, d), y_s.reshape(bs, ts, d),
            wk_p[None], wv_p[None], st_p[None],
            _heads(mk_p, MEM_HEADS)[None], _heads(mv_p, MEM_HEADS)[None],
            wk_s[None], wv_s[None], st_s[None])
```

```python
import functools

import jax
import jax.numpy as jnp
from jax import lax
from jax.experimental import pallas as pl
from jax.experimental.pallas import tpu as pltpu

F32 = jnp.float32
BF16 = jnp.bfloat16

D_MODEL = 4096
HEAD_DIM = 128
ATT_WIDTH = 2048
REC_WIDTH = 2048
N_ATT_HEADS = 16
N_REC_HEADS = 16
IN_COLS = 3 * ATT_WIDTH + 4 * REC_WIDTH
DILATED_PATTERNS = ((128, 1), (512, 4), (2048, 16))
WIN_BUF = 2048
PAST_LEN = 8192
ATT_SCALE = HEAD_DIM ** -0.5
ROT_HALF = HEAD_DIM // 8
ROPE_THETA = 500000.0
N_MEM = 256
MEM_HEADS = 4
MEM_WIDTH = 512
PEER_N_KEYS = 128
PEER_N_EXPERTS = PEER_N_KEYS * PEER_N_KEYS
PEER_HEADS = 8
PEER_TOPK = 16
EPS = 1e-6
NEG = -1e30

LANES = 128
VMEM_LIMIT = 56 * 1024 * 1024

NT_DIMS = (((1,), (1,)), ((), ()))
TN_DIMS = (((0,), (0,)), ((), ()))


def _params(*sem):
    return pltpu.CompilerParams(dimension_semantics=sem, vmem_limit_bytes=VMEM_LIMIT)


def _rms(x, g):
    return x * lax.rsqrt(jnp.mean(x * x, axis=-1, keepdims=True) + EPS) * g


def _norm_matmul_kernel(*refs, rope_tiles, heads_per_tile, emit_transposed):
    refs = list(refs)
    xn_ref = refs.pop()
    xt_ref = refs.pop() if emit_transposed else None
    if rope_tiles:
        x_ref, g_ref, w_ref, cos_ref, sa_ref, sb_ref, o_ref = refs
    else:
        x_ref, g_ref, w_ref, o_ref = refs
    j = pl.program_id(1)

    @pl.when(j == 0)
    def _():
        xn_ref[...] = _rms(x_ref[...], g_ref[...]).astype(BF16)
        if emit_transposed:
            xt_ref[...] = xn_ref[...].T

    o_ref[...] = jnp.dot(xn_ref[...], w_ref[...], preferred_element_type=F32).astype(o_ref.dtype)
    if rope_tiles:
        @pl.when(j < rope_tiles)
        def _():
            cos, sa, sb = cos_ref[...], sa_ref[...], sb_ref[...]
            for h in range(heads_per_tile):
                zh = o_ref[:, h * LANES:(h + 1) * LANES]
                o_ref[:, h * LANES:(h + 1) * LANES] = (
                    zh * cos + pltpu.roll(zh, LANES - ROT_HALF, 1) * sa + pltpu.roll(zh, ROT_HALF, 1) * sb)


def norm_matmul(x, g, w, *, rope=None, rope_cols=0, out_dtype=F32, tm=512, tn=512, return_normed=False):
    m, k = x.shape
    n = w.shape[1]
    tm, tn = min(tm, m), min(tn, n)
    assert m % tm == 0 and n % tn == 0 and rope_cols % tn == 0
    in_specs = [pl.BlockSpec((tm, k), lambda i, j: (i, 0)),
                pl.BlockSpec((1, k), lambda i, j: (0, 0)),
                pl.BlockSpec((k, tn), lambda i, j: (0, j))]
    args = [x, g.reshape(1, k), w]
    rope_tiles = 0
    if rope is not None:
        cos, sa, sb = rope
        period = cos.shape[0] // tm
        rope_tiles = rope_cols // tn
        spec = pl.BlockSpec((tm, LANES), lambda i, j: (i % period, 0))
        in_specs += [spec, spec, spec]
        args += [cos, sa, sb]
    out_specs = [pl.BlockSpec((tm, tn), lambda i, j: (i, j))]
    out_shape = [jax.ShapeDtypeStruct((m, n), out_dtype)]
    if return_normed:
        out_specs.append(pl.BlockSpec((k, tm), lambda i, j: (0, i)))
        out_shape.append(jax.ShapeDtypeStruct((k, m), BF16))
    out = pl.pallas_call(
        functools.partial(_norm_matmul_kernel, rope_tiles=rope_tiles, heads_per_tile=tn // LANES,
                          emit_transposed=return_normed),
        grid=(m // tm, n // tn),
        in_specs=in_specs,
        out_specs=out_specs,
        out_shape=out_shape,
        scratch_shapes=[pltpu.VMEM((tm, k), BF16)],
        compiler_params=_params("parallel", "arbitrary"),
        name="norm_matmul",
    )(*args)
    return out if return_normed else out[0]


def _rope_tables(pos):
    inv = ROPE_THETA ** (-jnp.arange(ROT_HALF, dtype=F32) / ROT_HALF)
    ang = pos.astype(F32)[:, None] * inv[None, :]
    cos, sin = jnp.cos(ang), jnp.sin(ang)
    t = pos.shape[0]
    rest = LANES - 2 * ROT_HALF
    cos_t = jnp.concatenate([cos, cos, jnp.ones((t, rest), F32)], axis=1)
    sa = jnp.concatenate([-sin, jnp.zeros((t, LANES - ROT_HALF), F32)], axis=1)
    sb = jnp.concatenate([jnp.zeros((t, ROT_HALF), F32), sin, jnp.zeros((t, rest), F32)], axis=1)
    return cos_t, sa, sb


def _matmul_residual_kernel(a_ref, w_ref, r_ref, o_ref):
    o_ref[...] = r_ref[...] + jnp.dot(a_ref[...], w_ref[...], preferred_element_type=F32)


def matmul_residual(a, w, res, *, tm=512, tn=512):
    m, k = a.shape
    n = w.shape[1]
    tm, tn = min(tm, m), min(tn, n)
    assert m % tm == 0 and n % tn == 0
    return pl.pallas_call(
        _matmul_residual_kernel,
        grid=(m // tm, n // tn),
        in_specs=[pl.BlockSpec((tm, k), lambda i, j: (i, 0)),
                  pl.BlockSpec((k, tn), lambda i, j: (0, j)),
                  pl.BlockSpec((tm, tn), lambda i, j: (i, j))],
        out_specs=pl.BlockSpec((tm, tn), lambda i, j: (i, j)),
        out_shape=jax.ShapeDtypeStruct((m, n), F32),
        compiler_params=_params("parallel", "parallel"),
        name="matmul_residual",
    )(a, w, res)


ATT_TQ = 256
ATT_BACK = WIN_BUF // ATT_TQ
ATT_WIN = WIN_BUF + ATT_TQ


def _distance_multiplicity(delta):
    c = jnp.zeros(delta.shape, F32)
    for window, dil in DILATED_PATTERNS:
        c = c + ((delta >= 0) & (delta <= window) & (delta % dil == 0)).astype(F32)
    return c


def _log_multiplicity(delta):
    c = _distance_multiplicity(delta)
    return jnp.where(c > 0, jnp.log(jnp.maximum(c, 1.0)), NEG)


def _attn_prompt_kernel(q_ref, k_ref, v_ref, g_ref, bias_ref, o_ref, kb_ref, vb_ref, s_ref):
    i = pl.program_id(2)

    @pl.when(i == 0)
    def _():
        kb_ref[0:WIN_BUF] = jnp.zeros((WIN_BUF, HEAD_DIM), BF16)
        vb_ref[0:WIN_BUF] = jnp.zeros((WIN_BUF, HEAD_DIM), BF16)
        kb_ref[WIN_BUF:] = k_ref[...].astype(BF16)
        vb_ref[WIN_BUF:] = v_ref[...].astype(BF16)

    q = (q_ref[...] * ATT_SCALE).astype(BF16)
    win = pl.ds(pl.multiple_of(i * ATT_TQ, ATT_TQ), ATT_WIN)
    s_ref[...] = lax.dot_general(q, kb_ref[win, :], NT_DIMS, preferred_element_type=F32) + bias_ref[...]

    @pl.when(i < ATT_BACK)
    def _():
        col = lax.broadcasted_iota(jnp.int32, (1, ATT_WIN), 1)
        s_ref[...] = jnp.where(col >= WIN_BUF - i * ATT_TQ, s_ref[...], NEG)

    s = s_ref[...]
    p = jnp.exp(s - jnp.max(s, axis=-1, keepdims=True))
    l = jnp.sum(p, axis=-1, keepdims=True)
    o = jnp.dot(p.astype(BF16), vb_ref[win, :], preferred_element_type=F32) / l
    o_ref[...] = _rms(o, g_ref[...]).astype(o_ref.dtype)


def attention_prompt(z, g_att):
    b, s, _ = z.shape
    rq = jnp.arange(ATT_TQ)[:, None]
    col = jnp.arange(ATT_WIN)[None, :]
    bias = _log_multiplicity(WIN_BUF + rq - col)
    kcol, vcol = ATT_WIDTH // HEAD_DIM, 2 * ATT_WIDTH // HEAD_DIM
    return pl.pallas_call(
        _attn_prompt_kernel,
        grid=(b, N_ATT_HEADS, s // ATT_TQ),
        in_specs=[pl.BlockSpec((None, ATT_TQ, HEAD_DIM), lambda bi, h, i: (bi, i, h)),
                  pl.BlockSpec((None, s, HEAD_DIM), lambda bi, h, i: (bi, 0, kcol + h)),
                  pl.BlockSpec((None, s, HEAD_DIM), lambda bi, h, i: (bi, 0, vcol + h)),
                  pl.BlockSpec((1, HEAD_DIM), lambda bi, h, i: (0, h)),
                  pl.BlockSpec((ATT_TQ, ATT_WIN), lambda bi, h, i: (0, 0))],
        out_specs=pl.BlockSpec((None, ATT_TQ, HEAD_DIM), lambda bi, h, i: (bi, i, h)),
        out_shape=jax.ShapeDtypeStruct((b, s, ATT_WIDTH), BF16),
        scratch_shapes=[pltpu.VMEM((WIN_BUF + s, HEAD_DIM), BF16), pltpu.VMEM((WIN_BUF + s, HEAD_DIM), BF16),
                        pltpu.VMEM((ATT_TQ, ATT_WIN), F32)],
        compiler_params=_params("parallel", "parallel", "arbitrary"),
        name="attention_prompt",
    )(z, z, z, g_att.reshape(1, ATT_WIDTH), bias)


SMP_TB = 512
SMP_ROWS = SMP_TB * N_ATT_HEADS


def _attn_sample_kernel(q_ref, kc_ref, vc_ref, kx_ref, vx_ref, kn_ref, vn_ref, bias_ref, biasn_ref, g_ref,
                        wk_ref, wv_ref, o_ref, m_ref, l_ref, acc_ref, *, t_new):
    kblk = pl.program_id(1)
    last = pl.num_programs(1) - 1

    @pl.when(kblk == 0)
    def _():
        m_ref[...] = jnp.full(m_ref.shape, NEG, F32)
        l_ref[...] = jnp.zeros(l_ref.shape, F32)
        acc_ref[...] = jnp.zeros(acc_ref.shape, F32)

    wk_ref[0:SMP_TB - t_new] = kc_ref[t_new:SMP_TB]
    wv_ref[0:SMP_TB - t_new] = vc_ref[t_new:SMP_TB]

    @pl.when(kblk < last)
    def _():
        wk_ref[SMP_TB - t_new:SMP_TB] = kx_ref[...]
        wv_ref[SMP_TB - t_new:SMP_TB] = vx_ref[...]

    @pl.when(kblk == last)
    def _():
        wk_ref[SMP_TB - t_new:SMP_TB] = kn_ref[...]
        wv_ref[SMP_TB - t_new:SMP_TB] = vn_ref[...]

    q = q_ref[...]

    def update(k2, v2, bias):
        s = lax.dot_general(q, k2.astype(BF16), NT_DIMS, preferred_element_type=F32) + bias
        m_old = m_ref[...]
        m_new = jnp.maximum(m_old, jnp.max(s, axis=-1, keepdims=True))
        alpha = jnp.exp(m_old - m_new)
        p = jnp.exp(s - m_new)
        l_ref[...] = alpha * l_ref[...] + jnp.sum(p, axis=-1, keepdims=True)
        acc_ref[...] = alpha * acc_ref[...] + jnp.dot(p.astype(BF16), v2.astype(BF16), preferred_element_type=F32)
        m_ref[...] = m_new

    update(kc_ref[...].reshape(SMP_ROWS, HEAD_DIM), vc_ref[...].reshape(SMP_ROWS, HEAD_DIM), bias_ref[...])

    @pl.when(kblk == last)
    def _():
        update(kn_ref[...].reshape(t_new * N_ATT_HEADS, HEAD_DIM), vn_ref[...].reshape(t_new * N_ATT_HEADS, HEAD_DIM),
               biasn_ref[...])
        o_ref[...] = _rms(acc_ref[...] / l_ref[...], g_ref[...])


def attention_sample(q, k_new, v_new, cache_k, cache_v, g_att):
    bd, t_new, nh, dh = q.shape
    nblk = WIN_BUF // SMP_TB
    qt = (q * ATT_SCALE).transpose(0, 2, 1, 3).reshape(bd, nh * t_new, dh).astype(BF16)
    hq = jnp.arange(nh * t_new)[:, None] // t_new
    jq = jnp.arange(nh * t_new)[:, None] % t_new
    def make_bias(tok, hk):
        return jnp.where(hq == hk, _log_multiplicity(WIN_BUF + jq - tok), NEG)
    cols = jnp.arange(WIN_BUF * nh)[None, :]
    bias = make_bias(cols // nh, cols % nh).reshape(nh * t_new, nblk, SMP_ROWS).transpose(1, 0, 2)
    cols_n = jnp.arange(t_new * nh)[None, :]
    bias_n = make_bias(WIN_BUF + cols_n // nh, cols_n % nh)
    g_rows = jnp.repeat(g_att.reshape(nh, dh), t_new, axis=0)
    sub = SMP_TB // t_new
    cache_blk = pl.BlockSpec((None, SMP_TB, nh, dh), lambda b, k: (b, k, 0, 0))
    next_blk = pl.BlockSpec((None, t_new, nh, dh), lambda b, k: (b, jnp.minimum((k + 1) * sub, WIN_BUF // t_new - 1), 0, 0))
    new_blk = pl.BlockSpec((None, t_new, nh, dh), lambda b, k: (b, 0, 0, 0))
    wk, wv, o = pl.pallas_call(
        functools.partial(_attn_sample_kernel, t_new=t_new),
        grid=(bd, nblk),
        in_specs=[pl.BlockSpec((None, nh * t_new, dh), lambda b, k: (b, 0, 0)),
                  cache_blk, cache_blk, next_blk, next_blk, new_blk, new_blk,
                  pl.BlockSpec((None, nh * t_new, SMP_ROWS), lambda b, k: (k, 0, 0)),
                  pl.BlockSpec((nh * t_new, nh * t_new), lambda b, k: (0, 0)),
                  pl.BlockSpec((nh * t_new, dh), lambda b, k: (0, 0))],
        out_specs=[cache_blk, cache_blk, pl.BlockSpec((None, nh * t_new, dh), lambda b, k: (b, 0, 0))],
        out_shape=[jax.ShapeDtypeStruct(cache_k.shape, F32), jax.ShapeDtypeStruct(cache_v.shape, F32),
                   jax.ShapeDtypeStruct((bd, nh * t_new, dh), F32)],
        scratch_shapes=[pltpu.VMEM((nh * t_new, 1), F32), pltpu.VMEM((nh * t_new, 1), F32),
                        pltpu.VMEM((nh * t_new, dh), F32)],
        compiler_params=_params("parallel", "arbitrary"),
        name="attention_sample",
    )(qt, cache_k, cache_v, cache_k, cache_v, k_new, v_new, bias, bias_n, g_rows)
    a = o.reshape(bd, nh, t_new, dh).transpose(0, 2, 1, 3).reshape(bd, t_new, nh * dh)
    return a, wk, wv


def _sigmoid_pair(x):
    e = jnp.exp(-jnp.abs(x))
    r = 1.0 / (1.0 + e)
    er = e * r
    pos = x >= 0
    return jnp.where(pos, r, er), jnp.where(pos, er, r)


REC_GROUP = 128
SUBLANES = 8


def _hgrn_kernel(q_ref, f_ref, i_ref, gz_ref, lb_ref, g_ref, s0_ref, r_ref, sfin_ref,
                 st_ref, oin_ref, qd_ref, u_ref, dec_ref, *, chunk, valid):
    lblk = pl.program_id(2)
    l_blk = q_ref.shape[0]
    grp = min(REC_GROUP, l_blk)
    per = grp // chunk

    @pl.when(lblk == 0)
    def _():
        for h in range(st_ref.shape[0]):
            st_ref[h] = s0_ref[h].T

    row = lax.broadcasted_iota(jnp.int32, (grp, grp), 0)
    col = lax.broadcasted_iota(jnp.int32, (grp, grp), 1)
    same_chunk_causal = (row // chunk == col // chunk) & (row >= col)
    pos = lax.broadcasted_iota(jnp.int32, (grp, HEAD_DIM), 0) % chunk

    def group_body(g, carry):
        rows = pl.ds(pl.multiple_of(g * grp, grp), grp)
        for h in range(st_ref.shape[0]):
            cols = slice(h * HEAD_DIM, (h + 1) * HEAD_DIM)
            lb = lb_ref[:, cols]
            sig, sig_neg = _sigmoid_pair(f_ref[rows, cols])
            logf = jnp.log(lb + (1.0 - lb) * sig)
            kr = (1.0 - lb) * sig_neg
            if valid < chunk:
                logf = jnp.where(pos < valid, logf, 0.0)
                kr = jnp.where(pos < valid, kr, 0.0)
            bcum = logf
            d = 1
            while d < chunk:
                bcum = bcum + jnp.where(pos >= d, pltpu.roll(bcum, d, 0), 0.0)
                d *= 2
            last = bcum.reshape(per, chunk, HEAD_DIM)[:, chunk - 1:chunk, :]
            blast = jnp.broadcast_to(last, (per, chunk, HEAD_DIM)).reshape(grp, HEAD_DIM)
            qz = q_ref[rows, cols]
            q = qz * _sigmoid_pair(qz)[0]
            v = i_ref[rows, cols].astype(BF16)
            q_dec = (q * jnp.exp(bcum)).astype(BF16)
            k_inv = (kr * jnp.exp(-bcum)).astype(BF16)
            k_dec = (kr * jnp.exp(blast - bcum)).astype(BF16)
            att = lax.dot_general(q_dec, k_inv, NT_DIMS, preferred_element_type=F32)
            att = jnp.where(same_chunk_causal, att, 0.0).astype(BF16)
            oin_ref[h, rows, :] = jnp.dot(att, v, preferred_element_type=F32)
            qd_ref[h, rows, :] = q_dec
            for c in range(per):
                cr = slice(c * chunk, (c + 1) * chunk)
                u_ref[h, g * per + c] = lax.dot_general(v[cr], k_dec[cr], TN_DIMS, preferred_element_type=F32)
                dec_ref[h, g * per + c] = jnp.broadcast_to(jnp.exp(last[c]), (SUBLANES, HEAD_DIM))
        return carry

    lax.fori_loop(0, l_blk // grp, group_body, 0)

    def chunk_body(c, carry):
        rows = pl.ds(pl.multiple_of(c * chunk, chunk), chunk)
        for h in range(st_ref.shape[0]):
            cols = slice(h * HEAD_DIM, (h + 1) * HEAD_DIM)
            st = st_ref[h]
            o = oin_ref[h, rows, :] + lax.dot_general(qd_ref[h, rows, :], st.astype(BF16), NT_DIMS,
                                                      preferred_element_type=F32)
            st3 = st.reshape(HEAD_DIM // SUBLANES, SUBLANES, HEAD_DIM) * dec_ref[h, c][None]
            st_ref[h] = st3.reshape(HEAD_DIM, HEAD_DIM) + u_ref[h, c]
            gz = gz_ref[rows, cols]
            gate = gz * _sigmoid_pair(gz)[0]
            r_ref[rows, cols] = (_rms(o, g_ref[:, cols]) * gate).astype(r_ref.dtype)
        return carry

    lax.fori_loop(0, l_blk // chunk, chunk_body, 0, unroll=min(4, l_blk // chunk))

    @pl.when(lblk == pl.num_programs(2) - 1)
    def _():
        for h in range(st_ref.shape[0]):
            sfin_ref[h] = st_ref[h].T


def hgrn(z, lb, g_rec, s0, *, chunk, valid, l_blk, heads, out_dtype):
    b, l, _ = z.shape
    assert l % l_blk == 0 and l_blk % chunk == 0 and N_REC_HEADS % heads == 0
    width = heads * HEAD_DIM
    base = 3 * ATT_WIDTH // width
    per = REC_WIDTH // width
    zspec = lambda k: pl.BlockSpec((None, l_blk, width), lambda bi, hg, li: (bi, li, base + k * per + hg))
    vec = pl.BlockSpec((1, width), lambda bi, hg, li: (0, hg))
    st = pl.BlockSpec((None, heads, HEAD_DIM, HEAD_DIM), lambda bi, hg, li: (bi, hg, 0, 0))
    return pl.pallas_call(
        functools.partial(_hgrn_kernel, chunk=chunk, valid=valid),
        grid=(b, N_REC_HEADS // heads, l // l_blk),
        in_specs=[zspec(0), zspec(1), zspec(2), zspec(3), vec, vec, st],
        out_specs=[pl.BlockSpec((None, l_blk, width), lambda bi, hg, li: (bi, li, hg)), st],
        out_shape=[jax.ShapeDtypeStruct((b, l, REC_WIDTH), out_dtype),
                   jax.ShapeDtypeStruct((b, N_REC_HEADS, HEAD_DIM, HEAD_DIM), F32)],
        scratch_shapes=[pltpu.VMEM((heads, HEAD_DIM, HEAD_DIM), F32),
                        pltpu.VMEM((heads, l_blk, HEAD_DIM), F32),
                        pltpu.VMEM((heads, l_blk, HEAD_DIM), BF16),
                        pltpu.VMEM((heads, l_blk // chunk, HEAD_DIM, HEAD_DIM), F32),
                        pltpu.VMEM((heads, l_blk // chunk, SUBLANES, HEAD_DIM), F32)],
        compiler_params=_params("parallel", "parallel", "arbitrary"),
        name="hgrn",
    )(z, z, z, z, lb.reshape(1, REC_WIDTH), g_rec.reshape(1, REC_WIDTH), s0)


def _mem_attn_kernel(q_ref, k_ref, v_ref, o_ref):
    scale = HEAD_DIM ** -0.5
    for h in range(MEM_HEADS):
        cols = slice(h * HEAD_DIM, (h + 1) * HEAD_DIM)
        q = (q_ref[:, cols] * scale).astype(BF16)
        s = lax.dot_general(q, k_ref[:, cols].astype(BF16), NT_DIMS, preferred_element_type=F32)
        p = jnp.exp(s - jnp.max(s, axis=-1, keepdims=True))
        l = jnp.sum(p, axis=-1, keepdims=True)
        o = jnp.dot(p.astype(BF16), v_ref[:, cols].astype(BF16), preferred_element_type=F32)
        o_ref[:, cols] = (o / l).astype(o_ref.dtype)


def mem_attention(q, mk, mv, *, tq):
    b, t, w = q.shape
    tq = min(tq, t)
    return pl.pallas_call(
        _mem_attn_kernel,
        grid=(b, t // tq),
        in_specs=[pl.BlockSpec((None, tq, w), lambda bi, i: (bi, i, 0)),
                  pl.BlockSpec((None, N_MEM, w), lambda bi, i: (bi, 0, 0)),
                  pl.BlockSpec((None, N_MEM, w), lambda bi, i: (bi, 0, 0))],
        out_specs=pl.BlockSpec((None, tq, w), lambda bi, i: (bi, i, 0)),
        out_shape=jax.ShapeDtypeStruct((b, t, w), BF16),
        compiler_params=_params("parallel", "parallel"),
        name="mem_attention",
    )(q, mk, mv)


PEER_TT = 512
PEER_MARK = 1e30
PEER_WIDE = 8


def _first_max(x, order, sentinel):
    m = jnp.max(x, axis=0, keepdims=True)
    return m, jnp.min(jnp.where(x == m, order, sentinel), axis=0, keepdims=True)


def _top16_cols(scores):
    n, t = scores[0].shape
    row = lax.broadcasted_iota(jnp.int32, (n, t), 0)
    row_out = lax.broadcasted_iota(jnp.int32, (PEER_TOPK, t), 0)

    def body(it, carry):
        mark = (lax.convert_element_type(it, F32) + 1.0) * -PEER_MARK
        out = []
        for s, vals in carry:
            m, first = _first_max(s, row, n)
            out.append((jnp.where(row == first, mark, s), jnp.where(row_out == it, m, vals)))
        return tuple(out)

    done = lax.fori_loop(0, PEER_TOPK, body, tuple((s, jnp.zeros((PEER_TOPK, t), F32)) for s in scores))
    return [(vals, jnp.where(s <= -0.5 * PEER_MARK, jnp.floor(s * (-1.0 / PEER_MARK) - 0.5), float(PEER_TOPK)))
            for s, vals in done]


def _candidates(v1, v2):
    parts = [v1[0:1, :] + v2]
    parts += [v1[r:r + 1, :] + v2[0:PEER_WIDE, :] for r in range(1, PEER_WIDE)]
    parts.append(v1[PEER_WIDE:, :] + v2[0:1, :])
    return jnp.concatenate(parts, axis=0)


def _peer_topk_kernel(q_ref, keys_ref, na_ref, e1z_ref, r2_ref, e2_ref):
    s1_all = lax.dot_general(keys_ref[0].astype(BF16), q_ref[:, 0:LANES].astype(BF16), NT_DIMS,
                             preferred_element_type=F32)
    s2_all = lax.dot_general(keys_ref[1].astype(BF16), q_ref[:, LANES:2 * LANES].astype(BF16), NT_DIMS,
                             preferred_element_type=F32)
    n_chains = q_ref.shape[0] // LANES
    stage1 = []
    for c in range(n_chains):
        lanes = slice(c * LANES, (c + 1) * LANES)
        stage1.append(_top16_cols((s1_all[:, lanes], s2_all[:, lanes])))

    head = PEER_TOPK + (PEER_WIDE - 1) * PEER_WIDE
    n_cand = head + PEER_TOPK - PEER_WIDE
    j = lax.broadcasted_iota(jnp.int32, (n_cand, LANES), 0)
    k = j - PEER_TOPK
    order = jnp.where(j < PEER_TOPK, j,
                      jnp.where(j < head, (1 + k // PEER_WIDE) * PEER_TOPK + k % PEER_WIDE,
                                (j - head + PEER_WIDE) * PEER_TOPK))

    def final_body(it, carry):
        out = []
        for cand, top0, zsum in carry:
            m, first = _first_max(cand, order, PEER_TOPK * PEER_TOPK)
            top0 = jnp.where(it == 0, m, top0)
            out.append((jnp.where(order == first, -jnp.inf, cand), top0, zsum + jnp.exp(m - top0)))
        return tuple(out)

    zeros = jnp.zeros((1, LANES), F32)
    final = lax.fori_loop(0, PEER_TOPK, final_body,
                          tuple((_candidates(v1, v2), zeros, zeros) for (v1, _), (v2, _) in stage1))

    for c in range(n_chains):
        lanes = slice(c * LANES, (c + 1) * LANES)
        (v1, r1), (v2, r2) = stage1[c]
        cand, _, zsum = final[c]
        sel = jnp.where(cand == -jnp.inf, 1.0, 0.0)
        na = jnp.zeros(r1.shape, F32)
        for r in range(PEER_TOPK):
            if r == 0:
                n_r = jnp.sum(sel[0:PEER_TOPK, :], axis=0, keepdims=True)
            elif r < PEER_WIDE:
                lo = PEER_TOPK + (r - 1) * PEER_WIDE
                n_r = jnp.sum(sel[lo:lo + PEER_WIDE, :], axis=0, keepdims=True)
            else:
                n_r = sel[head + r - PEER_WIDE:head + r - PEER_WIDE + 1, :]
            na = jnp.where(r1 == float(r), n_r, na)
        na_ref[:, lanes] = na
        r2_ref[:, lanes] = r2.astype(r2_ref.dtype)
        e1z_ref[:, lanes] = jnp.exp(s1_all[:, lanes] - v1[0:1, :]) / zsum
        e2_ref[:, lanes] = jnp.exp(s2_all[:, lanes] - v2[0:1, :]).astype(e2_ref.dtype)


def peer_topk(qry, sub_keys):
    m = qry.shape[0]
    tt = min(PEER_TT, m)
    out_spec = pl.BlockSpec((None, PEER_N_KEYS, tt), lambda i, h: (h, 0, i))
    shape = (PEER_HEADS, PEER_N_KEYS, m)
    return pl.pallas_call(
        _peer_topk_kernel,
        grid=(m // tt, PEER_HEADS),
        in_specs=[pl.BlockSpec((tt, 2 * PEER_N_KEYS), lambda i, h: (i, h)),
                  pl.BlockSpec((None, 2, PEER_N_KEYS, PEER_N_KEYS), lambda i, h: (h, 0, 0, 0))],
        out_specs=[out_spec, out_spec, out_spec, out_spec],
        out_shape=[jax.ShapeDtypeStruct(shape, F32), jax.ShapeDtypeStruct(shape, F32),
                   jax.ShapeDtypeStruct(shape, BF16), jax.ShapeDtypeStruct(shape, BF16)],
        compiler_params=_params("parallel", "parallel"),
        name="peer_topk",
    )(qry, sub_keys)


PEER_TE = 512
PEER_SUB = PEER_TE // PEER_N_KEYS


PEER_ROWS = 16

_ERF_T = 0.3275911
_ERF_C = (0.254829592, -0.284496736, 1.421413741, -1.453152027, 1.061405429)


def _gelu(x):
    z = x * (2.0 ** -0.5)
    t = 1.0 / (1.0 + _ERF_T * jnp.abs(z))
    half = 0.5 * _ERF_C[4]
    for c in _ERF_C[3::-1]:
        half = half * t + 0.5 * c
    half = half * t * jnp.exp(-(z * z))
    return x * jnp.where(z >= 0, 1.0 - half, half)


def _peer_mix_kernel(xt_ref, h_ref, gf_ref, u_ref, v_ref, na_ref, e1z_ref, r2_ref, e2_ref, o_ref, w_ref):
    e = pl.program_id(1)
    n_tiles = pl.num_programs(1) - 1
    tm, d = o_ref.shape
    groups = PEER_N_KEYS // PEER_ROWS

    @pl.when(e == 0)
    def _():
        o_ref[...] = jnp.zeros(o_ref.shape, F32)
        w_ref[...] = jnp.zeros(w_ref.shape, BF16)

    o_ref[...] += jnp.dot(w_ref[...], v_ref[...], preferred_element_type=F32)

    tile = jnp.minimum(e, n_tiles - 1)
    hid = jnp.dot(u_ref[...], xt_ref[...], preferred_element_type=F32)
    parts = []
    for sub in range(PEER_SUB):
        a = tile * PEER_SUB + sub
        act = _gelu(hid[sub * PEER_N_KEYS:(sub + 1) * PEER_N_KEYS, :]).astype(BF16)
        w = jnp.zeros((groups, PEER_ROWS, tm), BF16)
        for hd in range(PEER_HEADS):
            na = jnp.broadcast_to(na_ref[hd, pl.ds(a, 1), :], (PEER_ROWS, tm)).astype(BF16)
            e1z = jnp.broadcast_to(e1z_ref[hd, pl.ds(a, 1), :], (PEER_ROWS, tm)).astype(BF16)
            r2 = r2_ref[hd].reshape(groups, PEER_ROWS, tm)
            e2 = e2_ref[hd].reshape(groups, PEER_ROWS, tm)
            w = w + jnp.where(r2 < na[None], e2, jnp.zeros_like(e2)) * e1z[None]
        parts.append(w.reshape(PEER_N_KEYS, tm) * act)
    w_ref[...] = jnp.concatenate(parts, axis=0).T

    @pl.when(e == n_tiles)
    def _():
        o_ref[...] = _rms(h_ref[...] + o_ref[...], gf_ref[...])


def peer_mix(xt, h, g_final, u, v, na, e1z, r2, e2, *, tm):
    m, d = h.shape
    tm = min(tm, m)
    once = pl.Buffered(1)
    sel = pl.BlockSpec((PEER_HEADS, PEER_N_KEYS, tm), lambda i, e: (0, 0, i), pipeline_mode=once)
    n_tiles = PEER_N_EXPERTS // PEER_TE
    return pl.pallas_call(
        _peer_mix_kernel,
        grid=(m // tm, n_tiles + 1),
        in_specs=[pl.BlockSpec((d, tm), lambda i, e: (0, i), pipeline_mode=once),
                  pl.BlockSpec((tm, d), lambda i, e: (i, 0), pipeline_mode=once),
                  pl.BlockSpec((1, d), lambda i, e: (0, 0)),
                  pl.BlockSpec((PEER_TE, d), lambda i, e: (jnp.minimum(e, n_tiles - 1), 0)),
                  pl.BlockSpec((PEER_TE, d), lambda i, e: (jnp.maximum(e - 1, 0), 0)),
                  sel, sel, sel, sel],
        out_specs=pl.BlockSpec((tm, d), lambda i, e: (i, 0), pipeline_mode=once),
        out_shape=jax.ShapeDtypeStruct((m, d), F32),
        scratch_shapes=[pltpu.VMEM((tm, PEER_TE), BF16)],
        compiler_params=_params("parallel", "arbitrary"),
        name="peer_mix",
    )(xt, h, g_final.reshape(1, d), u, v, na, e1z, r2, e2)


def _heads(x, n):
    return x.reshape(x.shape[0], x.shape[1], n, HEAD_DIM)


def kernel(x_prompt, x_sample, cache_win_k, cache_win_v, state_hgrn, cache_mem_k, cache_mem_v, mem_prompt, norm_mix, w_in, lb_table, g_att_out, g_rec_out, w_out, norm_mem_x, norm_mem_kv, w_mem_q, w_mem_k, w_mem_v, w_mem_o, norm_ffn, peer_w_query, peer_sub_keys, peer_u, peer_v, norm_final):
    bp, sp, d = x_prompt.shape
    bs, ts, _ = x_sample.shape
    depth = w_in.shape[0]
    assert depth == 1
    rope_p = _rope_tables(jnp.arange(sp, dtype=jnp.int32))
    rope_s = _rope_tables(jnp.tile(PAST_LEN + jnp.arange(ts, dtype=jnp.int32), bs))
    lower_bounds = jnp.cumsum(jax.nn.softmax(lb_table.astype(F32), axis=0), axis=0)
    hp = x_prompt.reshape(bp * sp, d)
    hs = x_sample.reshape(bs * ts, d)
    l = 0
    lb = lower_bounds[l]
    w_in_b, w_out_b = w_in[l].astype(BF16), w_out[l].astype(BF16)

    zp = norm_matmul(hp, norm_mix[l], w_in_b, rope=rope_p, rope_cols=2 * ATT_WIDTH, tn=1024).reshape(bp, sp, IN_COLS)
    a_p = attention_prompt(zp, g_att_out[l])
    r_p, st_p = hgrn(zp, lb, g_rec_out[l], jnp.zeros((bp, N_REC_HEADS, HEAD_DIM, HEAD_DIM), F32),
                     chunk=32, valid=32, l_blk=1024, heads=4, out_dtype=BF16)
    mix_p = jnp.concatenate([a_p, r_p], axis=-1).reshape(bp * sp, d)
    hp = matmul_residual(mix_p, w_out_b, hp, tn=1024)
    wk_p = _heads(zp[:, sp - WIN_BUF:, ATT_WIDTH:2 * ATT_WIDTH], N_ATT_HEADS)
    wv_p = _heads(zp[:, sp - WIN_BUF:, 2 * ATT_WIDTH:3 * ATT_WIDTH], N_ATT_HEADS)

    zs = norm_matmul(hs, norm_mix[l], w_in_b, rope=rope_s, rope_cols=2 * ATT_WIDTH).reshape(bs, ts, IN_COLS)
    a_s, wk_s, wv_s = attention_sample(_heads(zs[..., :ATT_WIDTH], N_ATT_HEADS),
                                       _heads(zs[..., ATT_WIDTH:2 * ATT_WIDTH], N_ATT_HEADS),
                                       _heads(zs[..., 2 * ATT_WIDTH:3 * ATT_WIDTH], N_ATT_HEADS),
                                       cache_win_k[l], cache_win_v[l], g_att_out[l])
    pad_t = 16
    zs_pad = jnp.pad(zs, ((0, 0), (0, pad_t - ts), (0, 0)))
    r_s, st_s = hgrn(zs_pad, lb, g_rec_out[l], state_hgrn[l], chunk=pad_t, valid=ts, l_blk=pad_t,
                     heads=N_REC_HEADS, out_dtype=F32)
    mix_s = jnp.concatenate([a_s, r_s[:, :ts]], axis=-1).reshape(bs * ts, d).astype(BF16)
    hs = matmul_residual(mix_s, w_out_b, hs)

    w_kv = jnp.concatenate([w_mem_k[l], w_mem_v[l]], axis=1).astype(BF16)
    mkv = norm_matmul(mem_prompt.reshape(bp * N_MEM, d), norm_mem_kv[l], w_kv)
    mk_p = mkv[:, :MEM_WIDTH].reshape(bp, N_MEM, MEM_WIDTH)
    mv_p = mkv[:, MEM_WIDTH:].reshape(bp, N_MEM, MEM_WIDTH)
    w_q_b, w_o_b = w_mem_q[l].astype(BF16), w_mem_o[l].astype(BF16)
    qm_p = norm_matmul(hp, norm_mem_x[l], w_q_b).reshape(bp, sp, MEM_WIDTH)
    hp = matmul_residual(mem_attention(qm_p, mk_p, mv_p, tq=512).reshape(bp * sp, MEM_WIDTH), w_o_b, hp, tn=d)
    qm_s = norm_matmul(hs, norm_mem_x[l], w_q_b).reshape(bs, ts, MEM_WIDTH)
    qm_s = jnp.pad(qm_s, ((0, 0), (0, pad_t - ts), (0, 0)))
    om_s = mem_attention(qm_s, cache_mem_k[l].reshape(bs, N_MEM, MEM_WIDTH), cache_mem_v[l].reshape(bs, N_MEM, MEM_WIDTH), tq=pad_t)
    hs = matmul_residual(om_s[:, :ts].reshape(bs * ts, MEM_WIDTH), w_o_b, hs)

    w_pq = peer_w_query[l].astype(BF16)
    u_b, v_b = peer_u[l].astype(BF16), peer_v[l].astype(BF16)
    qry_p, xt_p = norm_matmul(hp, norm_ffn[l], w_pq, return_normed=True)
    y_p = peer_mix(xt_p, hp, norm_final, u_b, v_b, *peer_topk(qry_p, peer_sub_keys[l]), tm=512)
    qry_s, xt_s = norm_matmul(hs, norm_ffn[l], w_pq, return_normed=True)
    y_s = peer_mix(xt_s, hs, norm_final, u_b, v_b, *peer_topk(qry_s, peer_sub_keys[l]), tm=128)

    return (y_p.reshape(bp, sp, d), y_s.reshape(bs, ts, d),
            wk_p[None], wv_p[None], st_p[None],
            _heads(mk_p, MEM_HEADS)[None], _heads(mv_p, MEM_HEADS)[None],
            wk_s[None], wv_s[None], st_s[None])
```

```python
import functools

import jax
import jax.numpy as jnp
from jax import lax
from jax.experimental import pallas as pl
from jax.experimental.pallas import tpu as pltpu

F32 = jnp.float32
BF16 = jnp.bfloat16

D_MODEL = 4096
HEAD_DIM = 128
ATT_WIDTH = 2048
REC_WIDTH = 2048
N_ATT_HEADS = 16
N_REC_HEADS = 16
IN_COLS = 3 * ATT_WIDTH + 4 * REC_WIDTH
DILATED_PATTERNS = ((128, 1), (512, 4), (2048, 16))
WIN_BUF = 2048
PAST_LEN = 8192
ATT_SCALE = HEAD_DIM ** -0.5
ROT_HALF = HEAD_DIM // 8
ROPE_THETA = 500000.0
N_MEM = 256
MEM_HEADS = 4
MEM_WIDTH = 512
PEER_N_KEYS = 128
PEER_N_EXPERTS = PEER_N_KEYS * PEER_N_KEYS
PEER_HEADS = 8
PEER_TOPK = 16
EPS = 1e-6
NEG = -1e30

LANES = 128
VMEM_LIMIT = 56 * 1024 * 1024

NT_DIMS = (((1,), (1,)), ((), ()))
TN_DIMS = (((0,), (0,)), ((), ()))


def _params(*sem):
    return pltpu.CompilerParams(dimension_semantics=sem, vmem_limit_bytes=VMEM_LIMIT)


def _rms(x, g):
    return x * lax.rsqrt(jnp.mean(x * x, axis=-1, keepdims=True) + EPS) * g


def _norm_matmul_kernel(*refs, rope_tiles, heads_per_tile, emit_transposed):
    refs = list(refs)
    xn_ref = refs.pop()
    xt_ref = refs.pop() if emit_transposed else None
    if rope_tiles:
        x_ref, g_ref, w_ref, cos_ref, sa_ref, sb_ref, o_ref = refs
    else:
        x_ref, g_ref, w_ref, o_ref = refs
    j = pl.program_id(1)

    @pl.when(j == 0)
    def _():
        xn_ref[...] = _rms(x_ref[...], g_ref[...]).astype(BF16)
        if emit_transposed:
            xt_ref[...] = xn_ref[...].T

    o_ref[...] = jnp.dot(xn_ref[...], w_ref[...], preferred_element_type=F32).astype(o_ref.dtype)
    if rope_tiles:
        @pl.when(j < rope_tiles)
        def _():
            cos, sa, sb = cos_ref[...], sa_ref[...], sb_ref[...]
            for h in range(heads_per_tile):
                zh = o_ref[:, h * LANES:(h + 1) * LANES]
                o_ref[:, h * LANES:(h + 1) * LANES] = (
                    zh * cos + pltpu.roll(zh, LANES - ROT_HALF, 1) * sa + pltpu.roll(zh, ROT_HALF, 1) * sb)


def norm_matmul(x, g, w, *, rope=None, rope_cols=0, out_dtype=F32, tm=512, tn=512, return_normed=False):
    m, k = x.shape
    n = w.shape[1]
    tm, tn = min(tm, m), min(tn, n)
    assert m % tm == 0 and n % tn == 0 and rope_cols % tn == 0
    in_specs = [pl.BlockSpec((tm, k), lambda i, j: (i, 0)),
                pl.BlockSpec((1, k), lambda i, j: (0, 0)),
                pl.BlockSpec((k, tn), lambda i, j: (0, j))]
    args = [x, g.reshape(1, k), w]
    rope_tiles = 0
    if rope is not None:
        cos, sa, sb = rope
        period = cos.shape[0] // tm
        rope_tiles = rope_cols // tn
        spec = pl.BlockSpec((tm, LANES), lambda i, j: (i % period, 0))
        in_specs += [spec, spec, spec]
        args += [cos, sa, sb]
    out_specs = [pl.BlockSpec((tm, tn), lambda i, j: (i, j))]
    out_shape = [jax.ShapeDtypeStruct((m, n), out_dtype)]
    if return_normed:
        out_specs.append(pl.BlockSpec((k, tm), lambda i, j: (0, i)))
        out_shape.append(jax.ShapeDtypeStruct((k, m), BF16))
    out = pl.pallas_call(
        functools.partial(_norm_matmul_kernel, rope_tiles=rope_tiles, heads_per_tile=tn // LANES,
                          emit_transposed=return_normed),
        grid=(m // tm, n // tn),
        in_specs=in_specs,
        out_specs=out_specs,
        out_shape=out_shape,
        scratch_shapes=[pltpu.VMEM((tm, k), BF16)],
        compiler_params=_params("parallel", "arbitrary"),
        name="norm_matmul",
    )(*args)
    return out if return_normed else out[0]


def _rope_tables(pos):
    inv = ROPE_THETA ** (-jnp.arange(ROT_HALF, dtype=F32) / ROT_HALF)
    ang = pos.astype(F32)[:, None] * inv[None, :]
    cos, sin = jnp.cos(ang), jnp.sin(ang)
    t = pos.shape[0]
    rest = LANES - 2 * ROT_HALF
    cos_t = jnp.concatenate([cos, cos, jnp.ones((t, rest), F32)], axis=1)
    sa = jnp.concatenate([-sin, jnp.zeros((t, LANES - ROT_HALF), F32)], axis=1)
    sb = jnp.concatenate([jnp.zeros((t, ROT_HALF), F32), sin, jnp.zeros((t, rest), F32)], axis=1)
    return cos_t, sa, sb


def _matmul_residual_kernel(*refs):
    *a_refs, w_ref, r_ref, o_ref = refs
    acc = r_ref[...]
    row = 0
    for a_ref in a_refs:
        k = a_ref.shape[1]
        acc = acc + jnp.dot(a_ref[...], w_ref[row:row + k, :], preferred_element_type=F32)
        row += k
    o_ref[...] = acc


def matmul_residual(a, w, res, *, tm=512, tn=512):
    pieces = a if isinstance(a, (tuple, list)) else (a,)
    m = pieces[0].shape[0]
    k, n = w.shape
    tm, tn = min(tm, m), min(tn, n)
    assert m % tm == 0 and n % tn == 0 and sum(p.shape[1] for p in pieces) == k
    return pl.pallas_call(
        _matmul_residual_kernel,
        grid=(m // tm, n // tn),
        in_specs=[pl.BlockSpec((tm, p.shape[1]), lambda i, j: (i, 0)) for p in pieces]
        + [pl.BlockSpec((k, tn), lambda i, j: (0, j)),
           pl.BlockSpec((tm, tn), lambda i, j: (i, j))],
        out_specs=pl.BlockSpec((tm, tn), lambda i, j: (i, j)),
        out_shape=jax.ShapeDtypeStruct((m, n), F32),
        compiler_params=_params("parallel", "parallel"),
        name="matmul_residual",
    )(*pieces, w, res)


ATT_TQ = 256
ATT_BACK = WIN_BUF // ATT_TQ
ATT_WIN = WIN_BUF + ATT_TQ


def _distance_multiplicity(delta):
    c = jnp.zeros(delta.shape, F32)
    for window, dil in DILATED_PATTERNS:
        c = c + ((delta >= 0) & (delta <= window) & (delta % dil == 0)).astype(F32)
    return c


def _log_multiplicity(delta):
    c = _distance_multiplicity(delta)
    return jnp.where(c > 0, jnp.log(jnp.maximum(c, 1.0)), NEG)


def _attn_prompt_kernel(q_ref, k_ref, v_ref, g_ref, bias_ref, o_ref, kb_ref, vb_ref, s_ref):
    i = pl.program_id(2)

    @pl.when(i == 0)
    def _():
        kb_ref[0:WIN_BUF] = jnp.zeros((WIN_BUF, HEAD_DIM), BF16)
        vb_ref[0:WIN_BUF] = jnp.zeros((WIN_BUF, HEAD_DIM), BF16)
        kb_ref[WIN_BUF:] = k_ref[...].astype(BF16)
        vb_ref[WIN_BUF:] = v_ref[...].astype(BF16)

    q = (q_ref[...] * ATT_SCALE).astype(BF16)
    win = pl.ds(pl.multiple_of(i * ATT_TQ, ATT_TQ), ATT_WIN)
    s_ref[...] = lax.dot_general(q, kb_ref[win, :], NT_DIMS, preferred_element_type=F32) + bias_ref[...]

    @pl.when(i < ATT_BACK)
    def _():
        col = lax.broadcasted_iota(jnp.int32, (1, ATT_WIN), 1)
        s_ref[...] = jnp.where(col >= WIN_BUF - i * ATT_TQ, s_ref[...], NEG)

    s = s_ref[...]
    p = jnp.exp(s - jnp.max(s, axis=-1, keepdims=True))
    l = jnp.sum(p, axis=-1, keepdims=True)
    o = jnp.dot(p.astype(BF16), vb_ref[win, :], preferred_element_type=F32) / l
    o_ref[...] = _rms(o, g_ref[...]).astype(o_ref.dtype)


def attention_prompt(z, g_att):
    b, s, _ = z.shape
    rq = jnp.arange(ATT_TQ)[:, None]
    col = jnp.arange(ATT_WIN)[None, :]
    bias = _log_multiplicity(WIN_BUF + rq - col)
    kcol, vcol = ATT_WIDTH // HEAD_DIM, 2 * ATT_WIDTH // HEAD_DIM
    return pl.pallas_call(
        _attn_prompt_kernel,
        grid=(b, N_ATT_HEADS, s // ATT_TQ),
        in_specs=[pl.BlockSpec((None, ATT_TQ, HEAD_DIM), lambda bi, h, i: (bi, i, h)),
                  pl.BlockSpec((None, s, HEAD_DIM), lambda bi, h, i: (bi, 0, kcol + h)),
                  pl.BlockSpec((None, s, HEAD_DIM), lambda bi, h, i: (bi, 0, vcol + h)),
                  pl.BlockSpec((1, HEAD_DIM), lambda bi, h, i: (0, h)),
                  pl.BlockSpec((ATT_TQ, ATT_WIN), lambda bi, h, i: (0, 0))],
        out_specs=pl.BlockSpec((None, ATT_TQ, HEAD_DIM), lambda bi, h, i: (bi, i, h)),
        out_shape=jax.ShapeDtypeStruct((b, s, ATT_WIDTH), BF16),
        scratch_shapes=[pltpu.VMEM((WIN_BUF + s, HEAD_DIM), BF16), pltpu.VMEM((WIN_BUF + s, HEAD_DIM), BF16),
                        pltpu.VMEM((ATT_TQ, ATT_WIN), F32)],
        compiler_params=_params("parallel", "parallel", "arbitrary"),
        name="attention_prompt",
    )(z, z, z, g_att.reshape(1, ATT_WIDTH), bias)


SMP_TB = 512
SMP_ROWS = SMP_TB * N_ATT_HEADS


def _attn_sample_kernel(q_ref, kc_ref, vc_ref, kx_ref, vx_ref, kn_ref, vn_ref, bias_ref, biasn_ref, g_ref,
                        wk_ref, wv_ref, o_ref, m_ref, l_ref, acc_ref, *, t_new):
    kblk = pl.program_id(1)
    last = pl.num_programs(1) - 1

    @pl.when(kblk == 0)
    def _():
        m_ref[...] = jnp.full(m_ref.shape, NEG, F32)
        l_ref[...] = jnp.zeros(l_ref.shape, F32)
        acc_ref[...] = jnp.zeros(acc_ref.shape, F32)

    wk_ref[0:SMP_TB - t_new] = kc_ref[t_new:SMP_TB]
    wv_ref[0:SMP_TB - t_new] = vc_ref[t_new:SMP_TB]

    @pl.when(kblk < last)
    def _():
        wk_ref[SMP_TB - t_new:SMP_TB] = kx_ref[...]
        wv_ref[SMP_TB - t_new:SMP_TB] = vx_ref[...]

    @pl.when(kblk == last)
    def _():
        wk_ref[SMP_TB - t_new:SMP_TB] = kn_ref[...]
        wv_ref[SMP_TB - t_new:SMP_TB] = vn_ref[...]

    q = q_ref[...]

    def update(k2, v2, bias):
        s = lax.dot_general(q, k2.astype(BF16), NT_DIMS, preferred_element_type=F32) + bias
        m_old = m_ref[...]
        m_new = jnp.maximum(m_old, jnp.max(s, axis=-1, keepdims=True))
        alpha = jnp.exp(m_old - m_new)
        p = jnp.exp(s - m_new)
        l_ref[...] = alpha * l_ref[...] + jnp.sum(p, axis=-1, keepdims=True)
        acc_ref[...] = alpha * acc_ref[...] + jnp.dot(p.astype(BF16), v2.astype(BF16), preferred_element_type=F32)
        m_ref[...] = m_new

    update(kc_ref[...].reshape(SMP_ROWS, HEAD_DIM), vc_ref[...].reshape(SMP_ROWS, HEAD_DIM), bias_ref[...])

    @pl.when(kblk == last)
    def _():
        update(kn_ref[...].reshape(t_new * N_ATT_HEADS, HEAD_DIM), vn_ref[...].reshape(t_new * N_ATT_HEADS, HEAD_DIM),
               biasn_ref[...])
        o_ref[...] = _rms(acc_ref[...] / l_ref[...], g_ref[...])


def attention_sample(q, k_new, v_new, cache_k, cache_v, g_att):
    bd, t_new, nh, dh = q.shape
    nblk = WIN_BUF // SMP_TB
    qt = (q * ATT_SCALE).transpose(0, 2, 1, 3).reshape(bd, nh * t_new, dh).astype(BF16)
    hq = jnp.arange(nh * t_new)[:, None] // t_new
    jq = jnp.arange(nh * t_new)[:, None] % t_new
    def make_bias(tok, hk):
        return jnp.where(hq == hk, _log_multiplicity(WIN_BUF + jq - tok), NEG)
    cols = jnp.arange(WIN_BUF * nh)[None, :]
    bias = make_bias(cols // nh, cols % nh).reshape(nh * t_new, nblk, SMP_ROWS).transpose(1, 0, 2)
    cols_n = jnp.arange(t_new * nh)[None, :]
    bias_n = make_bias(WIN_BUF + cols_n // nh, cols_n % nh)
    g_rows = jnp.repeat(g_att.reshape(nh, dh), t_new, axis=0)
    sub = SMP_TB // t_new
    cache_blk = pl.BlockSpec((None, SMP_TB, nh, dh), lambda b, k: (b, k, 0, 0))
    next_blk = pl.BlockSpec((None, t_new, nh, dh), lambda b, k: (b, jnp.minimum((k + 1) * sub, WIN_BUF // t_new - 1), 0, 0))
    new_blk = pl.BlockSpec((None, t_new, nh, dh), lambda b, k: (b, 0, 0, 0))
    wk, wv, o = pl.pallas_call(
        functools.partial(_attn_sample_kernel, t_new=t_new),
        grid=(bd, nblk),
        in_specs=[pl.BlockSpec((None, nh * t_new, dh), lambda b, k: (b, 0, 0)),
                  cache_blk, cache_blk, next_blk, next_blk, new_blk, new_blk,
                  pl.BlockSpec((None, nh * t_new, SMP_ROWS), lambda b, k: (k, 0, 0)),
                  pl.BlockSpec((nh * t_new, nh * t_new), lambda b, k: (0, 0)),
                  pl.BlockSpec((nh * t_new, dh), lambda b, k: (0, 0))],
        out_specs=[cache_blk, cache_blk, pl.BlockSpec((None, nh * t_new, dh), lambda b, k: (b, 0, 0))],
        out_shape=[jax.ShapeDtypeStruct(cache_k.shape, F32), jax.ShapeDtypeStruct(cache_v.shape, F32),
                   jax.ShapeDtypeStruct((bd, nh * t_new, dh), F32)],
        scratch_shapes=[pltpu.VMEM((nh * t_new, 1), F32), pltpu.VMEM((nh * t_new, 1), F32),
                        pltpu.VMEM((nh * t_new, dh), F32)],
        compiler_params=_params("parallel", "arbitrary"),
        name="attention_sample",
    )(qt, cache_k, cache_v, cache_k, cache_v, k_new, v_new, bias, bias_n, g_rows)
    a = o.reshape(bd, nh, t_new, dh).transpose(0, 2, 1, 3).reshape(bd, t_new, nh * dh)
    return a, wk, wv


def _sigmoid_pair(x):
    e = jnp.exp(-jnp.abs(x))
    r = 1.0 / (1.0 + e)
    er = e * r
    pos = x >= 0
    return jnp.where(pos, r, er), jnp.where(pos, er, r)


REC_GROUP = 128
SUBLANES = 8


def _hgrn_kernel(q_ref, f_ref, i_ref, gz_ref, lb_ref, g_ref, s0_ref, r_ref, sfin_ref,
                 st_ref, oin_ref, qd_ref, u_ref, dec_ref, *, chunk, valid):
    lblk = pl.program_id(2)
    l_blk = q_ref.shape[0]
    grp = min(REC_GROUP, l_blk)
    per = grp // chunk

    @pl.when(lblk == 0)
    def _():
        for h in range(st_ref.shape[0]):
            st_ref[h] = s0_ref[h].T

    row = lax.broadcasted_iota(jnp.int32, (grp, grp), 0)
    col = lax.broadcasted_iota(jnp.int32, (grp, grp), 1)
    same_chunk_causal = (row // chunk == col // chunk) & (row >= col)
    pos = lax.broadcasted_iota(jnp.int32, (grp, HEAD_DIM), 0) % chunk

    def group_body(g, carry):
        rows = pl.ds(pl.multiple_of(g * grp, grp), grp)
        for h in range(st_ref.shape[0]):
            cols = slice(h * HEAD_DIM, (h + 1) * HEAD_DIM)
            lb = lb_ref[:, cols]
            sig, sig_neg = _sigmoid_pair(f_ref[rows, cols])
            logf = jnp.log(lb + (1.0 - lb) * sig)
            kr = (1.0 - lb) * sig_neg
            if valid < chunk:
                logf = jnp.where(pos < valid, logf, 0.0)
                kr = jnp.where(pos < valid, kr, 0.0)
            bcum = logf
            d = 1
            while d < chunk:
                bcum = bcum + jnp.where(pos >= d, pltpu.roll(bcum, d, 0), 0.0)
                d *= 2
            last = bcum.reshape(per, chunk, HEAD_DIM)[:, chunk - 1:chunk, :]
            blast = jnp.broadcast_to(last, (per, chunk, HEAD_DIM)).reshape(grp, HEAD_DIM)
            qz = q_ref[rows, cols]
            q = qz * _sigmoid_pair(qz)[0]
            v = i_ref[rows, cols].astype(BF16)
            q_dec = (q * jnp.exp(bcum)).astype(BF16)
            k_inv = (kr * jnp.exp(-bcum)).astype(BF16)
            k_dec = (kr * jnp.exp(blast - bcum)).astype(BF16)
            att = lax.dot_general(q_dec, k_inv, NT_DIMS, preferred_element_type=F32)
            att = jnp.where(same_chunk_causal, att, 0.0).astype(BF16)
            oin_ref[h, rows, :] = jnp.dot(att, v, preferred_element_type=F32)
            qd_ref[h, rows, :] = q_dec
            for c in range(per):
                cr = slice(c * chunk, (c + 1) * chunk)
                u_ref[h, g * per + c] = lax.dot_general(v[cr], k_dec[cr], TN_DIMS, preferred_element_type=F32)
                dec_ref[h, g * per + c] = jnp.broadcast_to(jnp.exp(last[c]), (SUBLANES, HEAD_DIM))
        return carry

    lax.fori_loop(0, l_blk // grp, group_body, 0)

    def chunk_body(c, carry):
        rows = pl.ds(pl.multiple_of(c * chunk, chunk), chunk)
        for h in range(st_ref.shape[0]):
            cols = slice(h * HEAD_DIM, (h + 1) * HEAD_DIM)
            st = st_ref[h]
            o = oin_ref[h, rows, :] + lax.dot_general(qd_ref[h, rows, :], st.astype(BF16), NT_DIMS,
                                                      preferred_element_type=F32)
            st3 = st.reshape(HEAD_DIM // SUBLANES, SUBLANES, HEAD_DIM) * dec_ref[h, c][None]
            st_ref[h] = st3.reshape(HEAD_DIM, HEAD_DIM) + u_ref[h, c]
            gz = gz_ref[rows, cols]
            gate = gz * _sigmoid_pair(gz)[0]
            r_ref[rows, cols] = (_rms(o, g_ref[:, cols]) * gate).astype(r_ref.dtype)
        return carry

    lax.fori_loop(0, l_blk // chunk, chunk_body, 0, unroll=min(4, l_blk // chunk))

    @pl.when(lblk == pl.num_programs(2) - 1)
    def _():
        for h in range(st_ref.shape[0]):
            sfin_ref[h] = st_ref[h].T


def hgrn(z, lb, g_rec, s0, *, chunk, valid, l_blk, heads, out_dtype):
    b, l, _ = z.shape
    assert l % l_blk == 0 and l_blk % chunk == 0 and N_REC_HEADS % heads == 0
    width = heads * HEAD_DIM
    base = 3 * ATT_WIDTH // width
    per = REC_WIDTH // width
    zspec = lambda k: pl.BlockSpec((None, l_blk, width), lambda bi, hg, li: (bi, li, base + k * per + hg))
    vec = pl.BlockSpec((1, width), lambda bi, hg, li: (0, hg))
    st = pl.BlockSpec((None, heads, HEAD_DIM, HEAD_DIM), lambda bi, hg, li: (bi, hg, 0, 0))
    return pl.pallas_call(
        functools.partial(_hgrn_kernel, chunk=chunk, valid=valid),
        grid=(b, N_REC_HEADS // heads, l // l_blk),
        in_specs=[zspec(0), zspec(1), zspec(2), zspec(3), vec, vec, st],
        out_specs=[pl.BlockSpec((None, l_blk, width), lambda bi, hg, li: (bi, li, hg)), st],
        out_shape=[jax.ShapeDtypeStruct((b, l, REC_WIDTH), out_dtype),
                   jax.ShapeDtypeStruct((b, N_REC_HEADS, HEAD_DIM, HEAD_DIM), F32)],
        scratch_shapes=[pltpu.VMEM((heads, HEAD_DIM, HEAD_DIM), F32),
                        pltpu.VMEM((heads, l_blk, HEAD_DIM), F32),
                        pltpu.VMEM((heads, l_blk, HEAD_DIM), BF16),
                        pltpu.VMEM((heads, l_blk // chunk, HEAD_DIM, HEAD_DIM), F32),
                        pltpu.VMEM((heads, l_blk // chunk, SUBLANES, HEAD_DIM), F32)],
        compiler_params=_params("parallel", "parallel", "arbitrary"),
        name="hgrn",
    )(z, z, z, z, lb.reshape(1, REC_WIDTH), g_rec.reshape(1, REC_WIDTH), s0)


def _mem_attn_kernel(q_ref, k_ref, v_ref, o_ref):
    scale = HEAD_DIM ** -0.5
    for h in range(MEM_HEADS):
        cols = slice(h * HEAD_DIM, (h + 1) * HEAD_DIM)
        q = (q_ref[:, cols] * scale).astype(BF16)
        s = lax.dot_general(q, k_ref[:, cols].astype(BF16), NT_DIMS, preferred_element_type=F32)
        p = jnp.exp(s - jnp.max(s, axis=-1, keepdims=True))
        l = jnp.sum(p, axis=-1, keepdims=True)
        o = jnp.dot(p.astype(BF16), v_ref[:, cols].astype(BF16), preferred_element_type=F32)
        o_ref[:, cols] = (o / l).astype(o_ref.dtype)


def _mem_heads(q, k_ref, v_ref):
    scale = HEAD_DIM ** -0.5
    outs = []
    for h in range(MEM_HEADS):
        cols = slice(h * HEAD_DIM, (h + 1) * HEAD_DIM)
        s = lax.dot_general((q[:, cols] * scale).astype(BF16), k_ref[:, cols].astype(BF16), NT_DIMS,
                            preferred_element_type=F32)
        p = jnp.exp(s - jnp.max(s, axis=-1, keepdims=True))
        l = jnp.sum(p, axis=-1, keepdims=True)
        o = jnp.dot(p.astype(BF16), v_ref[:, cols].astype(BF16), preferred_element_type=F32)
        outs.append((o / l).astype(BF16))
    return jnp.concatenate(outs, axis=1)


def _mem_block_kernel(h_ref, g_ref, wq_ref, k_ref, v_ref, wo_ref, o_ref):
    h = h_ref[...]
    q = jnp.dot(_rms(h, g_ref[...]).astype(BF16), wq_ref[...], preferred_element_type=F32)
    o_ref[...] = h + jnp.dot(_mem_heads(q, k_ref, v_ref), wo_ref[...], preferred_element_type=F32)


def mem_block(h, g, w_q, mk, mv, w_o, *, rows_per_batch, tm=512):
    m, d = h.shape
    w = w_q.shape[1]
    assert rows_per_batch % tm == 0
    per = rows_per_batch // tm
    once = pl.Buffered(1)
    return pl.pallas_call(
        _mem_block_kernel,
        grid=(m // tm,),
        in_specs=[pl.BlockSpec((tm, d), lambda i: (i, 0)),
                  pl.BlockSpec((1, d), lambda i: (0, 0)),
                  pl.BlockSpec((d, w), lambda i: (0, 0), pipeline_mode=once),
                  pl.BlockSpec((None, N_MEM, w), lambda i: (i // per, 0, 0)),
                  pl.BlockSpec((None, N_MEM, w), lambda i: (i // per, 0, 0)),
                  pl.BlockSpec((w, d), lambda i: (0, 0), pipeline_mode=once)],
        out_specs=pl.BlockSpec((tm, d), lambda i: (i, 0)),
        out_shape=jax.ShapeDtypeStruct((m, d), F32),
        compiler_params=_params("parallel"),
        name="mem_block",
    )(h, g.reshape(1, d), w_q, mk, mv, w_o)


def mem_attention(q, mk, mv, *, tq):
    b, t, w = q.shape
    tq = min(tq, t)
    return pl.pallas_call(
        _mem_attn_kernel,
        grid=(b, t // tq),
        in_specs=[pl.BlockSpec((None, tq, w), lambda bi, i: (bi, i, 0)),
                  pl.BlockSpec((None, N_MEM, w), lambda bi, i: (bi, 0, 0)),
                  pl.BlockSpec((None, N_MEM, w), lambda bi, i: (bi, 0, 0))],
        out_specs=pl.BlockSpec((None, tq, w), lambda bi, i: (bi, i, 0)),
        out_shape=jax.ShapeDtypeStruct((b, t, w), BF16),
        compiler_params=_params("parallel", "parallel"),
        name="mem_attention",
    )(q, mk, mv)


PEER_TT = 512
PEER_MARK = 1e30
PEER_WIDE = 8


def _first_max(x, order, sentinel):
    m = jnp.max(x, axis=0, keepdims=True)
    return m, jnp.min(jnp.where(x == m, order, sentinel), axis=0, keepdims=True)


def _top16_cols(scores):
    n, t = scores[0].shape
    row = lax.broadcasted_iota(jnp.int32, (n, t), 0)
    row_out = lax.broadcasted_iota(jnp.int32, (PEER_TOPK, t), 0)

    def body(it, carry):
        mark = (lax.convert_element_type(it, F32) + 1.0) * -PEER_MARK
        out = []
        for s, vals in carry:
            m, first = _first_max(s, row, n)
            out.append((jnp.where(row == first, mark, s), jnp.where(row_out == it, m, vals)))
        return tuple(out)

    done = lax.fori_loop(0, PEER_TOPK, body, tuple((s, jnp.zeros((PEER_TOPK, t), F32)) for s in scores))
    return [(vals, jnp.where(s <= -0.5 * PEER_MARK, jnp.floor(s * (-1.0 / PEER_MARK) - 0.5), float(PEER_TOPK)))
            for s, vals in done]


def _candidates(v1, v2):
    parts = [v1[0:1, :] + v2]
    parts += [v1[r:r + 1, :] + v2[0:PEER_WIDE, :] for r in range(1, PEER_WIDE)]
    parts.append(v1[PEER_WIDE:, :] + v2[0:1, :])
    return jnp.concatenate(parts, axis=0)


def _peer_topk_kernel(q_ref, keys_ref, na_ref, e1z_ref, r2_ref, e2_ref):
    s1_all = lax.dot_general(keys_ref[0].astype(BF16), q_ref[:, 0:LANES].astype(BF16), NT_DIMS,
                             preferred_element_type=F32)
    s2_all = lax.dot_general(keys_ref[1].astype(BF16), q_ref[:, LANES:2 * LANES].astype(BF16), NT_DIMS,
                             preferred_element_type=F32)
    n_chains = q_ref.shape[0] // LANES
    stage1 = []
    for c in range(n_chains):
        lanes = slice(c * LANES, (c + 1) * LANES)
        stage1.append(_top16_cols((s1_all[:, lanes], s2_all[:, lanes])))

    head = PEER_TOPK + (PEER_WIDE - 1) * PEER_WIDE
    n_cand = head + PEER_TOPK - PEER_WIDE
    j = lax.broadcasted_iota(jnp.int32, (n_cand, LANES), 0)
    k = j - PEER_TOPK
    order = jnp.where(j < PEER_TOPK, j,
                      jnp.where(j < head, (1 + k // PEER_WIDE) * PEER_TOPK + k % PEER_WIDE,
                                (j - head + PEER_WIDE) * PEER_TOPK))

    def final_body(it, carry):
        out = []
        for cand, top0, zsum in carry:
            m, first = _first_max(cand, order, PEER_TOPK * PEER_TOPK)
            top0 = jnp.where(it == 0, m, top0)
            out.append((jnp.where(order == first, -jnp.inf, cand), top0, zsum + jnp.exp(m - top0)))
        return tuple(out)

    zeros = jnp.zeros((1, LANES), F32)
    final = lax.fori_loop(0, PEER_TOPK, final_body,
                          tuple((_candidates(v1, v2), zeros, zeros) for (v1, _), (v2, _) in stage1))

    for c in range(n_chains):
        lanes = slice(c * LANES, (c + 1) * LANES)
        (v1, r1), (v2, r2) = stage1[c]
        cand, _, zsum = final[c]
        sel = jnp.where(cand == -jnp.inf, 1.0, 0.0)
        na = jnp.zeros(r1.shape, F32)
        for r in range(PEER_TOPK):
            if r == 0:
                n_r = jnp.sum(sel[0:PEER_TOPK, :], axis=0, keepdims=True)
            elif r < PEER_WIDE:
                lo = PEER_TOPK + (r - 1) * PEER_WIDE
                n_r = jnp.sum(sel[lo:lo + PEER_WIDE, :], axis=0, keepdims=True)
            else:
                n_r = sel[head + r - PEER_WIDE:head + r - PEER_WIDE + 1, :]
            na = jnp.where(r1 == float(r), n_r, na)
        na_ref[:, lanes] = na
        r2_ref[:, lanes] = r2.astype(r2_ref.dtype)
        e1z_ref[:, lanes] = jnp.exp(s1_all[:, lanes] - v1[0:1, :]) / zsum
        e2_ref[:, lanes] = jnp.exp(s2_all[:, lanes] - v2[0:1, :]).astype(e2_ref.dtype)


def peer_topk(qry, sub_keys):
    m = qry.shape[0]
    tt = min(PEER_TT, m)
    out_spec = pl.BlockSpec((None, PEER_N_KEYS, tt), lambda i, h: (h, 0, i))
    shape = (PEER_HEADS, PEER_N_KEYS, m)
    return pl.pallas_call(
        _peer_topk_kernel,
        grid=(m // tt, PEER_HEADS),
        in_specs=[pl.BlockSpec((tt, 2 * PEER_N_KEYS), lambda i, h: (i, h)),
                  pl.BlockSpec((None, 2, PEER_N_KEYS, PEER_N_KEYS), lambda i, h: (h, 0, 0, 0))],
        out_specs=[out_spec, out_spec, out_spec, out_spec],
        out_shape=[jax.ShapeDtypeStruct(shape, F32), jax.ShapeDtypeStruct(shape, F32),
                   jax.ShapeDtypeStruct(shape, BF16), jax.ShapeDtypeStruct(shape, BF16)],
        compiler_params=_params("parallel", "parallel"),
        name="peer_topk",
    )(qry, sub_keys)


PEER_TE = 512
PEER_SUB = PEER_TE // PEER_N_KEYS


PEER_ROWS = 16

_ERF_T = 0.3275911
_ERF_C = (0.254829592, -0.284496736, 1.421413741, -1.453152027, 1.061405429)


def _gelu(x):
    z = x * (2.0 ** -0.5)
    t = 1.0 / (1.0 + _ERF_T * jnp.abs(z))
    half = 0.5 * _ERF_C[4]
    for c in _ERF_C[3::-1]:
        half = half * t + 0.5 * c
    half = half * t * jnp.exp(-(z * z))
    return x * jnp.where(z >= 0, 1.0 - half, half)


def _peer_mix_kernel(xt_ref, h_ref, gf_ref, u_ref, v_ref, na_ref, e1z_ref, r2_ref, e2_ref, o_ref, w_ref):
    e = pl.program_id(1)
    n_tiles = pl.num_programs(1) - 1
    tm, d = o_ref.shape
    groups = PEER_N_KEYS // PEER_ROWS

    @pl.when(e == 0)
    def _():
        o_ref[...] = jnp.zeros(o_ref.shape, F32)
        w_ref[...] = jnp.zeros(w_ref.shape, BF16)

    o_ref[...] += jnp.dot(w_ref[...], v_ref[...], preferred_element_type=F32)

    tile = jnp.minimum(e, n_tiles - 1)
    hid = jnp.dot(u_ref[...], xt_ref[...], preferred_element_type=F32)
    parts = []
    for sub in range(PEER_SUB):
        a = tile * PEER_SUB + sub
        act = _gelu(hid[sub * PEER_N_KEYS:(sub + 1) * PEER_N_KEYS, :]).astype(BF16)
        w = jnp.zeros((groups, PEER_ROWS, tm), BF16)
        for hd in range(PEER_HEADS):
            na = jnp.broadcast_to(na_ref[hd, pl.ds(a, 1), :], (PEER_ROWS, tm)).astype(BF16)
            e1z = jnp.broadcast_to(e1z_ref[hd, pl.ds(a, 1), :], (PEER_ROWS, tm)).astype(BF16)
            r2 = r2_ref[hd].reshape(groups, PEER_ROWS, tm)
            e2 = e2_ref[hd].reshape(groups, PEER_ROWS, tm)
            w = w + jnp.where(r2 < na[None], e2, jnp.zeros_like(e2)) * e1z[None]
        parts.append(w.reshape(PEER_N_KEYS, tm) * act)
    w_ref[...] = jnp.concatenate(parts, axis=0).T

    @pl.when(e == n_tiles)
    def _():
        o_ref[...] = _rms(h_ref[...] + o_ref[...], gf_ref[...])


def peer_mix(xt, h, g_final, u, v, na, e1z, r2, e2, *, tm):
    m, d = h.shape
    tm = min(tm, m)
    once = pl.Buffered(1)
    sel = pl.BlockSpec((PEER_HEADS, PEER_N_KEYS, tm), lambda i, e: (0, 0, i), pipeline_mode=once)
    n_tiles = PEER_N_EXPERTS // PEER_TE
    return pl.pallas_call(
        _peer_mix_kernel,
        grid=(m // tm, n_tiles + 1),
        in_specs=[pl.BlockSpec((d, tm), lambda i, e: (0, i), pipeline_mode=once),
                  pl.BlockSpec((tm, d), lambda i, e: (i, 0), pipeline_mode=once),
                  pl.BlockSpec((1, d), lambda i, e: (0, 0)),
                  pl.BlockSpec((PEER_TE, d), lambda i, e: (jnp.minimum(e, n_tiles - 1), 0)),
                  pl.BlockSpec((PEER_TE, d), lambda i, e: (jnp.maximum(e - 1, 0), 0)),
                  sel, sel, sel, sel],
        out_specs=pl.BlockSpec((tm, d), lambda i, e: (i, 0), pipeline_mode=once),
        out_shape=jax.ShapeDtypeStruct((m, d), F32),
        scratch_shapes=[pltpu.VMEM((tm, PEER_TE), BF16)],
        compiler_params=_params("parallel", "arbitrary"),
        name="peer_mix",
    )(xt, h, g_final.reshape(1, d), u, v, na, e1z, r2, e2)


def _heads(x, n):
    return x.reshape(x.shape[0], x.shape[1], n, HEAD_DIM)


def kernel(x_prompt, x_sample, cache_win_k, cache_win_v, state_hgrn, cache_mem_k, cache_mem_v, mem_prompt, norm_mix, w_in, lb_table, g_att_out, g_rec_out, w_out, norm_mem_x, norm_mem_kv, w_mem_q, w_mem_k, w_mem_v, w_mem_o, norm_ffn, peer_w_query, peer_sub_keys, peer_u, peer_v, norm_final):
    bp, sp, d = x_prompt.shape
    bs, ts, _ = x_sample.shape
    depth = w_in.shape[0]
    assert depth == 1
    rope_p = _rope_tables(jnp.arange(sp, dtype=jnp.int32))
    rope_s = _rope_tables(jnp.tile(PAST_LEN + jnp.arange(ts, dtype=jnp.int32), bs))
    lower_bounds = jnp.cumsum(jax.nn.softmax(lb_table.astype(F32), axis=0), axis=0)
    hp = x_prompt.reshape(bp * sp, d)
    hs = x_sample.reshape(bs * ts, d)
    l = 0
    lb = lower_bounds[l]
    w_in_b, w_out_b = w_in[l].astype(BF16), w_out[l].astype(BF16)

    zp = norm_matmul(hp, norm_mix[l], w_in_b, rope=rope_p, rope_cols=2 * ATT_WIDTH, tn=1024).reshape(bp, sp, IN_COLS)
    a_p = attention_prompt(zp, g_att_out[l])
    r_p, st_p = hgrn(zp, lb, g_rec_out[l], jnp.zeros((bp, N_REC_HEADS, HEAD_DIM, HEAD_DIM), F32),
                     chunk=32, valid=32, l_blk=1024, heads=4, out_dtype=BF16)
    hp = matmul_residual((a_p.reshape(bp * sp, ATT_WIDTH), r_p.reshape(bp * sp, REC_WIDTH)), w_out_b, hp, tn=1024)
    wk_p = _heads(zp[:, sp - WIN_BUF:, ATT_WIDTH:2 * ATT_WIDTH], N_ATT_HEADS)
    wv_p = _heads(zp[:, sp - WIN_BUF:, 2 * ATT_WIDTH:3 * ATT_WIDTH], N_ATT_HEADS)

    zs = norm_matmul(hs, norm_mix[l], w_in_b, rope=rope_s, rope_cols=2 * ATT_WIDTH).reshape(bs, ts, IN_COLS)
    a_s, wk_s, wv_s = attention_sample(_heads(zs[..., :ATT_WIDTH], N_ATT_HEADS),
                                       _heads(zs[..., ATT_WIDTH:2 * ATT_WIDTH], N_ATT_HEADS),
                                       _heads(zs[..., 2 * ATT_WIDTH:3 * ATT_WIDTH], N_ATT_HEADS),
                                       cache_win_k[l], cache_win_v[l], g_att_out[l])
    pad_t = 16
    zs_pad = jnp.pad(zs, ((0, 0), (0, pad_t - ts), (0, 0)))
    r_s, st_s = hgrn(zs_pad, lb, g_rec_out[l], state_hgrn[l], chunk=pad_t, valid=ts, l_blk=pad_t,
                     heads=N_REC_HEADS, out_dtype=F32)
    mix_s = jnp.concatenate([a_s, r_s[:, :ts]], axis=-1).reshape(bs * ts, d).astype(BF16)
    hs = matmul_residual(mix_s, w_out_b, hs)

    w_kv = jnp.concatenate([w_mem_k[l], w_mem_v[l]], axis=1).astype(BF16)
    mkv = norm_matmul(mem_prompt.reshape(bp * N_MEM, d), norm_mem_kv[l], w_kv)
    mk_p = mkv[:, :MEM_WIDTH].reshape(bp, N_MEM, MEM_WIDTH)
    mv_p = mkv[:, MEM_WIDTH:].reshape(bp, N_MEM, MEM_WIDTH)
    w_q_b, w_o_b = w_mem_q[l].astype(BF16), w_mem_o[l].astype(BF16)
    hp = mem_block(hp, norm_mem_x[l], w_q_b, mk_p, mv_p, w_o_b, rows_per_batch=sp)
    qm_s = norm_matmul(hs, norm_mem_x[l], w_q_b).reshape(bs, ts, MEM_WIDTH)
    qm_s = jnp.pad(qm_s, ((0, 0), (0, pad_t - ts), (0, 0)))
    om_s = mem_attention(qm_s, cache_mem_k[l].reshape(bs, N_MEM, MEM_WIDTH), cache_mem_v[l].reshape(bs, N_MEM, MEM_WIDTH), tq=pad_t)
    hs = matmul_residual(om_s[:, :ts].reshape(bs * ts, MEM_WIDTH), w_o_b, hs)

    w_pq = peer_w_query[l].astype(BF16)
    u_b, v_b = peer_u[l].astype(BF16), peer_v[l].astype(BF16)
    qry_p, xt_p = norm_matmul(hp, norm_ffn[l], w_pq, return_normed=True)
    y_p = peer_mix(xt_p, hp, norm_final, u_b, v_b, *peer_topk(qry_p, peer_sub_keys[l]), tm=512)
    qry_s, xt_s = norm_matmul(hs, norm_ffn[l], w_pq, return_normed=True)
    y_s = peer_mix(xt_s, hs, norm_final, u_b, v_b, *peer_topk(qry_s, peer_sub_keys[l]), tm=128)

    return (y_p.reshape(bp, sp, d), y_s.reshape(bs, ts, d),
            wk_p[None], wv_p[None], st_p[None],
            _heads(mk_p, MEM_HEADS)[None], _heads(mv_p, MEM_HEADS)[None],
            wk_s[None], wv_s[None], st_s[None])
```

```python
import functools

import jax
import jax.numpy as jnp
import numpy as np
from jax import lax
from jax.experimental import pallas as pl
from jax.experimental.pallas import tpu as pltpu

F32 = jnp.float32
BF16 = jnp.bfloat16

D_MODEL = 4096
HEAD_DIM = 128
ATT_WIDTH = 2048
REC_WIDTH = 2048
N_ATT_HEADS = 16
N_REC_HEADS = 16
IN_COLS = 3 * ATT_WIDTH + 4 * REC_WIDTH
DILATED_PATTERNS = ((128, 1), (512, 4), (2048, 16))
WIN_BUF = 2048
PAST_LEN = 8192
ATT_SCALE = HEAD_DIM ** -0.5
ROT_HALF = HEAD_DIM // 8
ROPE_THETA = 500000.0
N_MEM = 256
MEM_HEADS = 4
MEM_WIDTH = 512
PEER_N_KEYS = 128
PEER_N_EXPERTS = PEER_N_KEYS * PEER_N_KEYS
PEER_HEADS = 8
PEER_TOPK = 16
EPS = 1e-6
NEG = -1e30

LANES = 128
VMEM_LIMIT = 56 * 1024 * 1024

NT_DIMS = (((1,), (1,)), ((), ()))
TN_DIMS = (((0,), (0,)), ((), ()))


def _params(*sem):
    return pltpu.CompilerParams(dimension_semantics=sem, vmem_limit_bytes=VMEM_LIMIT)


def _rms(x, g):
    return x * lax.rsqrt(jnp.mean(x * x, axis=-1, keepdims=True) + EPS) * g


def _norm_matmul_kernel(*refs, rope_tiles, heads_per_tile, emit_transposed):
    refs = list(refs)
    xn_ref = refs.pop()
    xt_ref = refs.pop() if emit_transposed else None
    if rope_tiles:
        x_ref, g_ref, w_ref, cos_ref, sa_ref, sb_ref, o_ref = refs
    else:
        x_ref, g_ref, w_ref, o_ref = refs
    j = pl.program_id(1)

    @pl.when(j == 0)
    def _():
        xn_ref[...] = _rms(x_ref[...], g_ref[...]).astype(BF16)
        if emit_transposed:
            xt_ref[...] = xn_ref[...].T

    o_ref[...] = jnp.dot(xn_ref[...], w_ref[...], preferred_element_type=F32).astype(o_ref.dtype)
    if rope_tiles:
        @pl.when(j < rope_tiles)
        def _():
            cos, sa, sb = cos_ref[...], sa_ref[...], sb_ref[...]
            for h in range(heads_per_tile):
                zh = o_ref[:, h * LANES:(h + 1) * LANES]
                o_ref[:, h * LANES:(h + 1) * LANES] = (
                    zh * cos + pltpu.roll(zh, LANES - ROT_HALF, 1) * sa + pltpu.roll(zh, ROT_HALF, 1) * sb)


def norm_matmul(x, g, w, *, rope=None, rope_cols=0, out_dtype=F32, tm=512, tn=512, return_normed=False):
    m, k = x.shape
    n = w.shape[1]
    tm, tn = min(tm, m), min(tn, n)
    assert m % tm == 0 and n % tn == 0 and rope_cols % tn == 0
    in_specs = [pl.BlockSpec((tm, k), lambda i, j: (i, 0)),
                pl.BlockSpec((1, k), lambda i, j: (0, 0)),
                pl.BlockSpec((k, tn), lambda i, j: (0, j))]
    args = [x, g.reshape(1, k), w]
    rope_tiles = 0
    if rope is not None:
        cos, sa, sb = rope
        period = cos.shape[0] // tm
        rope_tiles = rope_cols // tn
        spec = pl.BlockSpec((tm, LANES), lambda i, j: (i % period, 0))
        in_specs += [spec, spec, spec]
        args += [cos, sa, sb]
    out_specs = [pl.BlockSpec((tm, tn), lambda i, j: (i, j))]
    out_shape = [jax.ShapeDtypeStruct((m, n), out_dtype)]
    if return_normed:
        out_specs.append(pl.BlockSpec((k, tm), lambda i, j: (0, i)))
        out_shape.append(jax.ShapeDtypeStruct((k, m), BF16))
    out = pl.pallas_call(
        functools.partial(_norm_matmul_kernel, rope_tiles=rope_tiles, heads_per_tile=tn // LANES,
                          emit_transposed=return_normed),
        grid=(m // tm, n // tn),
        in_specs=in_specs,
        out_specs=out_specs,
        out_shape=out_shape,
        scratch_shapes=[pltpu.VMEM((tm, k), BF16)],
        compiler_params=_params("parallel", "arbitrary"),
        name="norm_matmul",
    )(*args)
    return out if return_normed else out[0]


def _rope_tables(pos):
    inv = ROPE_THETA ** (-jnp.arange(ROT_HALF, dtype=F32) / ROT_HALF)
    ang = pos.astype(F32)[:, None] * inv[None, :]
    cos, sin = jnp.cos(ang), jnp.sin(ang)
    t = pos.shape[0]
    rest = LANES - 2 * ROT_HALF
    cos_t = jnp.concatenate([cos, cos, jnp.ones((t, rest), F32)], axis=1)
    sa = jnp.concatenate([-sin, jnp.zeros((t, LANES - ROT_HALF), F32)], axis=1)
    sb = jnp.concatenate([jnp.zeros((t, ROT_HALF), F32), sin, jnp.zeros((t, rest), F32)], axis=1)
    return cos_t, sa, sb


def _matmul_residual_kernel(*refs):
    *a_refs, w_ref, r_ref, o_ref = refs
    acc = r_ref[...]
    row = 0
    for a_ref in a_refs:
        k = a_ref.shape[1]
        acc = acc + jnp.dot(a_ref[...], w_ref[row:row + k, :], preferred_element_type=F32)
        row += k
    o_ref[...] = acc


def matmul_residual(a, w, res, *, tm=512, tn=512):
    pieces = a if isinstance(a, (tuple, list)) else (a,)
    m = pieces[0].shape[0]
    k, n = w.shape
    tm, tn = min(tm, m), min(tn, n)
    assert m % tm == 0 and n % tn == 0 and sum(p.shape[1] for p in pieces) == k
    return pl.pallas_call(
        _matmul_residual_kernel,
        grid=(m // tm, n // tn),
        in_specs=[pl.BlockSpec((tm, p.shape[1]), lambda i, j: (i, 0)) for p in pieces]
        + [pl.BlockSpec((k, tn), lambda i, j: (0, j)),
           pl.BlockSpec((tm, tn), lambda i, j: (i, j))],
        out_specs=pl.BlockSpec((tm, tn), lambda i, j: (i, j)),
        out_shape=jax.ShapeDtypeStruct((m, n), F32),
        compiler_params=_params("parallel", "parallel"),
        name="matmul_residual",
    )(*pieces, w, res)


ATT_TQ = 256
ATT_BACK = WIN_BUF // ATT_TQ
ATT_WIN = WIN_BUF + ATT_TQ


def _distance_multiplicity(delta):
    c = np.zeros(delta.shape, np.float32)
    for window, dil in DILATED_PATTERNS:
        c = c + ((delta >= 0) & (delta <= window) & (delta % dil == 0)).astype(np.float32)
    return c


def _log_multiplicity(delta):
    c = _distance_multiplicity(delta)
    return np.where(c > 0, np.log(np.maximum(c, 1.0)), NEG).astype(np.float32)


def _attn_prompt_kernel(q_ref, k_ref, v_ref, g_ref, bias_ref, o_ref, kb_ref, vb_ref, s_ref):
    i = pl.program_id(2)

    @pl.when(i == 0)
    def _():
        kb_ref[0:WIN_BUF] = jnp.zeros((WIN_BUF, HEAD_DIM), BF16)
        vb_ref[0:WIN_BUF] = jnp.zeros((WIN_BUF, HEAD_DIM), BF16)
        kb_ref[WIN_BUF:] = k_ref[...].astype(BF16)
        vb_ref[WIN_BUF:] = v_ref[...].astype(BF16)

    q = (q_ref[...] * ATT_SCALE).astype(BF16)
    win = pl.ds(pl.multiple_of(i * ATT_TQ, ATT_TQ), ATT_WIN)
    s_ref[...] = lax.dot_general(q, kb_ref[win, :], NT_DIMS, preferred_element_type=F32) + bias_ref[...]

    @pl.when(i < ATT_BACK)
    def _():
        col = lax.broadcasted_iota(jnp.int32, (1, ATT_WIN), 1)
        s_ref[...] = jnp.where(col >= WIN_BUF - i * ATT_TQ, s_ref[...], NEG)

    s = s_ref[...]
    p = jnp.exp(s - jnp.max(s, axis=-1, keepdims=True))
    l = jnp.sum(p, axis=-1, keepdims=True)
    o = jnp.dot(p.astype(BF16), vb_ref[win, :], preferred_element_type=F32) / l
    o_ref[...] = _rms(o, g_ref[...]).astype(o_ref.dtype)


def attention_prompt(z, g_att):
    b, s, _ = z.shape
    rq = np.arange(ATT_TQ)[:, None]
    col = np.arange(ATT_WIN)[None, :]
    bias = _log_multiplicity(WIN_BUF + rq - col)
    kcol, vcol = ATT_WIDTH // HEAD_DIM, 2 * ATT_WIDTH // HEAD_DIM
    return pl.pallas_call(
        _attn_prompt_kernel,
        grid=(b, N_ATT_HEADS, s // ATT_TQ),
        in_specs=[pl.BlockSpec((None, ATT_TQ, HEAD_DIM), lambda bi, h, i: (bi, i, h)),
                  pl.BlockSpec((None, s, HEAD_DIM), lambda bi, h, i: (bi, 0, kcol + h)),
                  pl.BlockSpec((None, s, HEAD_DIM), lambda bi, h, i: (bi, 0, vcol + h)),
                  pl.BlockSpec((1, HEAD_DIM), lambda bi, h, i: (0, h)),
                  pl.BlockSpec((ATT_TQ, ATT_WIN), lambda bi, h, i: (0, 0))],
        out_specs=pl.BlockSpec((None, ATT_TQ, HEAD_DIM), lambda bi, h, i: (bi, i, h)),
        out_shape=jax.ShapeDtypeStruct((b, s, ATT_WIDTH), BF16),
        scratch_shapes=[pltpu.VMEM((WIN_BUF + s, HEAD_DIM), BF16), pltpu.VMEM((WIN_BUF + s, HEAD_DIM), BF16),
                        pltpu.VMEM((ATT_TQ, ATT_WIN), F32)],
        compiler_params=_params("parallel", "parallel", "arbitrary"),
        name="attention_prompt",
    )(z, z, z, g_att.reshape(1, ATT_WIDTH), bias)


SMP_TB = 512
SMP_ROWS = SMP_TB * N_ATT_HEADS


def _attn_sample_kernel(q_ref, kc_ref, vc_ref, kx_ref, vx_ref, kn_ref, vn_ref, bias_ref, biasn_ref, g_ref,
                        wk_ref, wv_ref, o_ref, m_ref, l_ref, acc_ref, *, t_new):
    kblk = pl.program_id(1)
    last = pl.num_programs(1) - 1

    @pl.when(kblk == 0)
    def _():
        m_ref[...] = jnp.full(m_ref.shape, NEG, F32)
        l_ref[...] = jnp.zeros(l_ref.shape, F32)
        acc_ref[...] = jnp.zeros(acc_ref.shape, F32)

    wk_ref[0:SMP_TB - t_new] = kc_ref[t_new:SMP_TB]
    wv_ref[0:SMP_TB - t_new] = vc_ref[t_new:SMP_TB]

    @pl.when(kblk < last)
    def _():
        wk_ref[SMP_TB - t_new:SMP_TB] = kx_ref[...]
        wv_ref[SMP_TB - t_new:SMP_TB] = vx_ref[...]

    @pl.when(kblk == last)
    def _():
        wk_ref[SMP_TB - t_new:SMP_TB] = kn_ref[...]
        wv_ref[SMP_TB - t_new:SMP_TB] = vn_ref[...]

    q = q_ref[...]

    def update(k2, v2, bias):
        s = lax.dot_general(q, k2.astype(BF16), NT_DIMS, preferred_element_type=F32) + bias
        m_old = m_ref[...]
        m_new = jnp.maximum(m_old, jnp.max(s, axis=-1, keepdims=True))
        alpha = jnp.exp(m_old - m_new)
        p = jnp.exp(s - m_new)
        l_ref[...] = alpha * l_ref[...] + jnp.sum(p, axis=-1, keepdims=True)
        acc_ref[...] = alpha * acc_ref[...] + jnp.dot(p.astype(BF16), v2.astype(BF16), preferred_element_type=F32)
        m_ref[...] = m_new

    update(kc_ref[...].reshape(SMP_ROWS, HEAD_DIM), vc_ref[...].reshape(SMP_ROWS, HEAD_DIM), bias_ref[...])

    @pl.when(kblk == last)
    def _():
        update(kn_ref[...].reshape(t_new * N_ATT_HEADS, HEAD_DIM), vn_ref[...].reshape(t_new * N_ATT_HEADS, HEAD_DIM),
               biasn_ref[...])
        o_ref[...] = _rms(acc_ref[...] / l_ref[...], g_ref[...])


def attention_sample(q, k_new, v_new, cache_k, cache_v, g_att):
    bd, t_new, nh, dh = q.shape
    nblk = WIN_BUF // SMP_TB
    qt = (q * ATT_SCALE).transpose(0, 2, 1, 3).reshape(bd, nh * t_new, dh).astype(BF16)
    hq = np.arange(nh * t_new)[:, None] // t_new
    jq = np.arange(nh * t_new)[:, None] % t_new
    def make_bias(tok, hk):
        return np.where(hq == hk, _log_multiplicity(WIN_BUF + jq - tok), np.float32(NEG))
    cols = np.arange(WIN_BUF * nh)[None, :]
    bias = make_bias(cols // nh, cols % nh).reshape(nh * t_new, nblk, SMP_ROWS).transpose(1, 0, 2)
    cols_n = np.arange(t_new * nh)[None, :]
    bias_n = make_bias(WIN_BUF + cols_n // nh, cols_n % nh)
    g_rows = jnp.repeat(g_att.reshape(nh, dh), t_new, axis=0)
    sub = SMP_TB // t_new
    cache_blk = pl.BlockSpec((None, SMP_TB, nh, dh), lambda b, k: (b, k, 0, 0))
    next_blk = pl.BlockSpec((None, t_new, nh, dh), lambda b, k: (b, jnp.minimum((k + 1) * sub, WIN_BUF // t_new - 1), 0, 0))
    new_blk = pl.BlockSpec((None, t_new, nh, dh), lambda b, k: (b, 0, 0, 0))
    wk, wv, o = pl.pallas_call(
        functools.partial(_attn_sample_kernel, t_new=t_new),
        grid=(bd, nblk),
        in_specs=[pl.BlockSpec((None, nh * t_new, dh), lambda b, k: (b, 0, 0)),
                  cache_blk, cache_blk, next_blk, next_blk, new_blk, new_blk,
                  pl.BlockSpec((None, nh * t_new, SMP_ROWS), lambda b, k: (k, 0, 0)),
                  pl.BlockSpec((nh * t_new, nh * t_new), lambda b, k: (0, 0)),
                  pl.BlockSpec((nh * t_new, dh), lambda b, k: (0, 0))],
        out_specs=[cache_blk, cache_blk, pl.BlockSpec((None, nh * t_new, dh), lambda b, k: (b, 0, 0))],
        out_shape=[jax.ShapeDtypeStruct(cache_k.shape, F32), jax.ShapeDtypeStruct(cache_v.shape, F32),
                   jax.ShapeDtypeStruct((bd, nh * t_new, dh), F32)],
        scratch_shapes=[pltpu.VMEM((nh * t_new, 1), F32), pltpu.VMEM((nh * t_new, 1), F32),
                        pltpu.VMEM((nh * t_new, dh), F32)],
        compiler_params=_params("parallel", "arbitrary"),
        name="attention_sample",
    )(qt, cache_k, cache_v, cache_k, cache_v, k_new, v_new, bias, bias_n, g_rows)
    a = o.reshape(bd, nh, t_new, dh).transpose(0, 2, 1, 3).reshape(bd, t_new, nh * dh)
    return a, wk, wv


def _sigmoid_pair(x):
    e = jnp.exp(-jnp.abs(x))
    r = 1.0 / (1.0 + e)
    er = e * r
    pos = x >= 0
    return jnp.where(pos, r, er), jnp.where(pos, er, r)


REC_GROUP = 128
SUBLANES = 8


def _hgrn_kernel(q_ref, f_ref, i_ref, gz_ref, lb_ref, g_ref, s0_ref, r_ref, sfin_ref,
                 st_ref, oin_ref, qd_ref, u_ref, dec_ref, *, chunk, valid):
    lblk = pl.program_id(2)
    l_blk = q_ref.shape[0]
    grp = min(REC_GROUP, l_blk)
    per = grp // chunk

    @pl.when(lblk == 0)
    def _():
        for h in range(st_ref.shape[0]):
            st_ref[h] = s0_ref[h].T

    row = lax.broadcasted_iota(jnp.int32, (grp, grp), 0)
    col = lax.broadcasted_iota(jnp.int32, (grp, grp), 1)
    same_chunk_causal = (row // chunk == col // chunk) & (row >= col)
    pos = lax.broadcasted_iota(jnp.int32, (grp, HEAD_DIM), 0) % chunk

    def group_body(g, carry):
        rows = pl.ds(pl.multiple_of(g * grp, grp), grp)
        for h in range(st_ref.shape[0]):
            cols = slice(h * HEAD_DIM, (h + 1) * HEAD_DIM)
            lb = lb_ref[:, cols]
            sig, sig_neg = _sigmoid_pair(f_ref[rows, cols])
            logf = jnp.log(lb + (1.0 - lb) * sig)
            kr = (1.0 - lb) * sig_neg
            if valid < chunk:
                logf = jnp.where(pos < valid, logf, 0.0)
                kr = jnp.where(pos < valid, kr, 0.0)
            bcum = logf
            d = 1
            while d < chunk:
                bcum = bcum + jnp.where(pos >= d, pltpu.roll(bcum, d, 0), 0.0)
                d *= 2
            last = bcum.reshape(per, chunk, HEAD_DIM)[:, chunk - 1:chunk, :]
            blast = jnp.broadcast_to(last, (per, chunk, HEAD_DIM)).reshape(grp, HEAD_DIM)
            qz = q_ref[rows, cols]
            q = qz * _sigmoid_pair(qz)[0]
            v = i_ref[rows, cols].astype(BF16)
            q_dec = (q * jnp.exp(bcum)).astype(BF16)
            k_inv = (kr * jnp.exp(-bcum)).astype(BF16)
            k_dec = (kr * jnp.exp(blast - bcum)).astype(BF16)
            att = lax.dot_general(q_dec, k_inv, NT_DIMS, preferred_element_type=F32)
            att = jnp.where(same_chunk_causal, att, 0.0).astype(BF16)
            oin_ref[h, rows, :] = jnp.dot(att, v, preferred_element_type=F32)
            qd_ref[h, rows, :] = q_dec
            for c in range(per):
                cr = slice(c * chunk, (c + 1) * chunk)
                u_ref[h, g * per + c] = lax.dot_general(v[cr], k_dec[cr], TN_DIMS, preferred_element_type=F32)
                dec_ref[h, g * per + c] = jnp.broadcast_to(jnp.exp(last[c]), (SUBLANES, HEAD_DIM))
        return carry

    lax.fori_loop(0, l_blk // grp, group_body, 0)

    def chunk_body(c, carry):
        rows = pl.ds(pl.multiple_of(c * chunk, chunk), chunk)
        for h in range(st_ref.shape[0]):
            cols = slice(h * HEAD_DIM, (h + 1) * HEAD_DIM)
            st = st_ref[h]
            o = oin_ref[h, rows, :] + lax.dot_general(qd_ref[h, rows, :], st.astype(BF16), NT_DIMS,
                                                      preferred_element_type=F32)
            st3 = st.reshape(HEAD_DIM // SUBLANES, SUBLANES, HEAD_DIM) * dec_ref[h, c][None]
            st_ref[h] = st3.reshape(HEAD_DIM, HEAD_DIM) + u_ref[h, c]
            gz = gz_ref[rows, cols]
            gate = gz * _sigmoid_pair(gz)[0]
            r_ref[rows, cols] = (_rms(o, g_ref[:, cols]) * gate).astype(r_ref.dtype)
        return carry

    lax.fori_loop(0, l_blk // chunk, chunk_body, 0, unroll=min(4, l_blk // chunk))

    @pl.when(lblk == pl.num_programs(2) - 1)
    def _():
        for h in range(st_ref.shape[0]):
            sfin_ref[h] = st_ref[h].T


def hgrn(z, lb, g_rec, s0, *, chunk, valid, l_blk, heads, out_dtype):
    b, l, _ = z.shape
    assert l % l_blk == 0 and l_blk % chunk == 0 and N_REC_HEADS % heads == 0
    width = heads * HEAD_DIM
    base = 3 * ATT_WIDTH // width
    per = REC_WIDTH // width
    zspec = lambda k: pl.BlockSpec((None, l_blk, width), lambda bi, hg, li: (bi, li, base + k * per + hg))
    vec = pl.BlockSpec((1, width), lambda bi, hg, li: (0, hg))
    st = pl.BlockSpec((None, heads, HEAD_DIM, HEAD_DIM), lambda bi, hg, li: (bi, hg, 0, 0))
    return pl.pallas_call(
        functools.partial(_hgrn_kernel, chunk=chunk, valid=valid),
        grid=(b, N_REC_HEADS // heads, l // l_blk),
        in_specs=[zspec(0), zspec(1), zspec(2), zspec(3), vec, vec, st],
        out_specs=[pl.BlockSpec((None, l_blk, width), lambda bi, hg, li: (bi, li, hg)), st],
        out_shape=[jax.ShapeDtypeStruct((b, l, REC_WIDTH), out_dtype),
                   jax.ShapeDtypeStruct((b, N_REC_HEADS, HEAD_DIM, HEAD_DIM), F32)],
        scratch_shapes=[pltpu.VMEM((heads, HEAD_DIM, HEAD_DIM), F32),
                        pltpu.VMEM((heads, l_blk, HEAD_DIM), F32),
                        pltpu.VMEM((heads, l_blk, HEAD_DIM), BF16),
                        pltpu.VMEM((heads, l_blk // chunk, HEAD_DIM, HEAD_DIM), F32),
                        pltpu.VMEM((heads, l_blk // chunk, SUBLANES, HEAD_DIM), F32)],
        compiler_params=_params("parallel", "parallel", "arbitrary"),
        name="hgrn",
    )(z, z, z, z, lb.reshape(1, REC_WIDTH), g_rec.reshape(1, REC_WIDTH), s0)


def _mem_attn_kernel(q_ref, k_ref, v_ref, o_ref):
    scale = HEAD_DIM ** -0.5
    for h in range(MEM_HEADS):
        cols = slice(h * HEAD_DIM, (h + 1) * HEAD_DIM)
        q = (q_ref[:, cols] * scale).astype(BF16)
        s = lax.dot_general(q, k_ref[:, cols].astype(BF16), NT_DIMS, preferred_element_type=F32)
        p = jnp.exp(s - jnp.max(s, axis=-1, keepdims=True))
        l = jnp.sum(p, axis=-1, keepdims=True)
        o = jnp.dot(p.astype(BF16), v_ref[:, cols].astype(BF16), preferred_element_type=F32)
        o_ref[:, cols] = (o / l).astype(o_ref.dtype)


def _mem_heads(q, k_ref, v_ref):
    scale = HEAD_DIM ** -0.5
    outs = []
    for h in range(MEM_HEADS):
        cols = slice(h * HEAD_DIM, (h + 1) * HEAD_DIM)
        s = lax.dot_general((q[:, cols] * scale).astype(BF16), k_ref[:, cols].astype(BF16), NT_DIMS,
                            preferred_element_type=F32)
        p = jnp.exp(s - jnp.max(s, axis=-1, keepdims=True))
        l = jnp.sum(p, axis=-1, keepdims=True)
        o = jnp.dot(p.astype(BF16), v_ref[:, cols].astype(BF16), preferred_element_type=F32)
        outs.append((o / l).astype(BF16))
    return jnp.concatenate(outs, axis=1)


def _mem_block_kernel(h_ref, g_ref, wq_ref, k_ref, v_ref, wo_ref, o_ref):
    h = h_ref[...]
    q = jnp.dot(_rms(h, g_ref[...]).astype(BF16), wq_ref[...], preferred_element_type=F32)
    o_ref[...] = h + jnp.dot(_mem_heads(q, k_ref, v_ref), wo_ref[...], preferred_element_type=F32)


def mem_block(h, g, w_q, mk, mv, w_o, *, rows_per_batch, tm=512):
    m, d = h.shape
    w = w_q.shape[1]
    assert rows_per_batch % tm == 0
    per = rows_per_batch // tm
    once = pl.Buffered(1)
    return pl.pallas_call(
        _mem_block_kernel,
        grid=(m // tm,),
        in_specs=[pl.BlockSpec((tm, d), lambda i: (i, 0)),
                  pl.BlockSpec((1, d), lambda i: (0, 0)),
                  pl.BlockSpec((d, w), lambda i: (0, 0), pipeline_mode=once),
                  pl.BlockSpec((None, N_MEM, w), lambda i: (i // per, 0, 0)),
                  pl.BlockSpec((None, N_MEM, w), lambda i: (i // per, 0, 0)),
                  pl.BlockSpec((w, d), lambda i: (0, 0), pipeline_mode=once)],
        out_specs=pl.BlockSpec((tm, d), lambda i: (i, 0)),
        out_shape=jax.ShapeDtypeStruct((m, d), F32),
        compiler_params=_params("parallel"),
        name="mem_block",
    )(h, g.reshape(1, d), w_q, mk, mv, w_o)


def mem_attention(q, mk, mv, *, tq):
    b, t, w = q.shape
    tq = min(tq, t)
    return pl.pallas_call(
        _mem_attn_kernel,
        grid=(b, t // tq),
        in_specs=[pl.BlockSpec((None, tq, w), lambda bi, i: (bi, i, 0)),
                  pl.BlockSpec((None, N_MEM, w), lambda bi, i: (bi, 0, 0)),
                  pl.BlockSpec((None, N_MEM, w), lambda bi, i: (bi, 0, 0))],
        out_specs=pl.BlockSpec((None, tq, w), lambda bi, i: (bi, i, 0)),
        out_shape=jax.ShapeDtypeStruct((b, t, w), BF16),
        compiler_params=_params("parallel", "parallel"),
        name="mem_attention",
    )(q, mk, mv)


PEER_TT = 512
PEER_MARK = 1e30
PEER_WIDE = 8


def _first_max(x, order, sentinel):
    m = jnp.max(x, axis=0, keepdims=True)
    return m, jnp.min(jnp.where(x == m, order, sentinel), axis=0, keepdims=True)


def _top16_cols(scores):
    n, t = scores[0].shape
    row = lax.broadcasted_iota(jnp.int32, (n, t), 0)
    row_out = lax.broadcasted_iota(jnp.int32, (PEER_TOPK, t), 0)

    def body(it, carry):
        mark = (lax.convert_element_type(it, F32) + 1.0) * -PEER_MARK
        out = []
        for s, vals in carry:
            m, first = _first_max(s, row, n)
            out.append((jnp.where(row == first, mark, s), jnp.where(row_out == it, m, vals)))
        return tuple(out)

    done = lax.fori_loop(0, PEER_TOPK, body, tuple((s, jnp.zeros((PEER_TOPK, t), F32)) for s in scores))
    return [(vals, jnp.where(s <= -0.5 * PEER_MARK, jnp.floor(s * (-1.0 / PEER_MARK) - 0.5), float(PEER_TOPK)))
            for s, vals in done]


def _candidates(v1, v2):
    parts = [v1[0:1, :] + v2]
    parts += [v1[r:r + 1, :] + v2[0:PEER_WIDE, :] for r in range(1, PEER_WIDE)]
    parts.append(v1[PEER_WIDE:, :] + v2[0:1, :])
    return jnp.concatenate(parts, axis=0)


def _peer_topk_kernel(q_ref, keys_ref, na_ref, e1z_ref, r2_ref, e2_ref):
    s1_all = lax.dot_general(keys_ref[0].astype(BF16), q_ref[:, 0:LANES].astype(BF16), NT_DIMS,
                             preferred_element_type=F32)
    s2_all = lax.dot_general(keys_ref[1].astype(BF16), q_ref[:, LANES:2 * LANES].astype(BF16), NT_DIMS,
                             preferred_element_type=F32)
    n_chains = q_ref.shape[0] // LANES
    stage1 = []
    for c in range(n_chains):
        lanes = slice(c * LANES, (c + 1) * LANES)
        stage1.append(_top16_cols((s1_all[:, lanes], s2_all[:, lanes])))

    head = PEER_TOPK + (PEER_WIDE - 1) * PEER_WIDE
    n_cand = head + PEER_TOPK - PEER_WIDE
    j = lax.broadcasted_iota(jnp.int32, (n_cand, LANES), 0)
    k = j - PEER_TOPK
    order = jnp.where(j < PEER_TOPK, j,
                      jnp.where(j < head, (1 + k // PEER_WIDE) * PEER_TOPK + k % PEER_WIDE,
                                (j - head + PEER_WIDE) * PEER_TOPK))

    def final_body(it, carry):
        out = []
        for cand, top0, zsum in carry:
            m, first = _first_max(cand, order, PEER_TOPK * PEER_TOPK)
            top0 = jnp.where(it == 0, m, top0)
            out.append((jnp.where(order == first, -jnp.inf, cand), top0, zsum + jnp.exp(m - top0)))
        return tuple(out)

    zeros = jnp.zeros((1, LANES), F32)
    final = lax.fori_loop(0, PEER_TOPK, final_body,
                          tuple((_candidates(v1, v2), zeros, zeros) for (v1, _), (v2, _) in stage1))

    for c in range(n_chains):
        lanes = slice(c * LANES, (c + 1) * LANES)
        (v1, r1), (v2, r2) = stage1[c]
        cand, _, zsum = final[c]
        sel = jnp.where(cand == -jnp.inf, 1.0, 0.0)
        na = jnp.zeros(r1.shape, F32)
        for r in range(PEER_TOPK):
            if r == 0:
                n_r = jnp.sum(sel[0:PEER_TOPK, :], axis=0, keepdims=True)
            elif r < PEER_WIDE:
                lo = PEER_TOPK + (r - 1) * PEER_WIDE
                n_r = jnp.sum(sel[lo:lo + PEER_WIDE, :], axis=0, keepdims=True)
            else:
                n_r = sel[head + r - PEER_WIDE:head + r - PEER_WIDE + 1, :]
            na = jnp.where(r1 == float(r), n_r, na)
        na_ref[:, lanes] = na
        r2_ref[:, lanes] = r2.astype(r2_ref.dtype)
        e1z_ref[:, lanes] = jnp.exp(s1_all[:, lanes] - v1[0:1, :]) / zsum
        e2_ref[:, lanes] = jnp.exp(s2_all[:, lanes] - v2[0:1, :]).astype(e2_ref.dtype)


def peer_topk(qry, sub_keys):
    m = qry.shape[0]
    tt = min(PEER_TT, m)
    out_spec = pl.BlockSpec((None, PEER_N_KEYS, tt), lambda i, h: (h, 0, i))
    shape = (PEER_HEADS, PEER_N_KEYS, m)
    return pl.pallas_call(
        _peer_topk_kernel,
        grid=(m // tt, PEER_HEADS),
        in_specs=[pl.BlockSpec((tt, 2 * PEER_N_KEYS), lambda i, h: (i, h)),
                  pl.BlockSpec((None, 2, PEER_N_KEYS, PEER_N_KEYS), lambda i, h: (h, 0, 0, 0))],
        out_specs=[out_spec, out_spec, out_spec, out_spec],
        out_shape=[jax.ShapeDtypeStruct(shape, F32), jax.ShapeDtypeStruct(shape, F32),
                   jax.ShapeDtypeStruct(shape, BF16), jax.ShapeDtypeStruct(shape, BF16)],
        compiler_params=_params("parallel", "parallel"),
        name="peer_topk",
    )(qry, sub_keys)


PEER_TE = 512
PEER_SUB = PEER_TE // PEER_N_KEYS


PEER_ROWS = 16

_ERF_T = 0.3275911
_ERF_C = (0.254829592, -0.284496736, 1.421413741, -1.453152027, 1.061405429)


def _gelu(x):
    z = x * (2.0 ** -0.5)
    t = 1.0 / (1.0 + _ERF_T * jnp.abs(z))
    half = 0.5 * _ERF_C[4]
    for c in _ERF_C[3::-1]:
        half = half * t + 0.5 * c
    half = half * t * jnp.exp(-(z * z))
    return x * jnp.where(z >= 0, 1.0 - half, half)


def _peer_mix_kernel(xt_ref, h_ref, gf_ref, u_ref, v_ref, na_ref, e1z_ref, r2_ref, e2_ref, o_ref, w_ref):
    e = pl.program_id(1)
    n_tiles = pl.num_programs(1) - 1
    tm, d = o_ref.shape
    groups = PEER_N_KEYS // PEER_ROWS

    @pl.when(e == 0)
    def _():
        o_ref[...] = jnp.zeros(o_ref.shape, F32)
        w_ref[...] = jnp.zeros(w_ref.shape, BF16)

    o_ref[...] += jnp.dot(w_ref[...], v_ref[...], preferred_element_type=F32)

    tile = jnp.minimum(e, n_tiles - 1)
    hid = jnp.dot(u_ref[...], xt_ref[...], preferred_element_type=F32)
    parts = []
    for sub in range(PEER_SUB):
        a = tile * PEER_SUB + sub
        act = _gelu(hid[sub * PEER_N_KEYS:(sub + 1) * PEER_N_KEYS, :]).astype(BF16)
        w = jnp.zeros((groups, PEER_ROWS, tm), BF16)
        for hd in range(PEER_HEADS):
            na = jnp.broadcast_to(na_ref[hd, pl.ds(a, 1), :], (PEER_ROWS, tm)).astype(BF16)
            e1z = jnp.broadcast_to(e1z_ref[hd, pl.ds(a, 1), :], (PEER_ROWS, tm)).astype(BF16)
            r2 = r2_ref[hd].reshape(groups, PEER_ROWS, tm)
            e2 = e2_ref[hd].reshape(groups, PEER_ROWS, tm)
            w = w + jnp.where(r2 < na[None], e2, jnp.zeros_like(e2)) * e1z[None]
        parts.append(w.reshape(PEER_N_KEYS, tm) * act)
    w_ref[...] = jnp.concatenate(parts, axis=0).T

    @pl.when(e == n_tiles)
    def _():
        o_ref[...] = _rms(h_ref[...] + o_ref[...], gf_ref[...])


def peer_mix(xt, h, g_final, u, v, na, e1z, r2, e2, *, tm):
    m, d = h.shape
    tm = min(tm, m)
    once = pl.Buffered(1)
    sel = pl.BlockSpec((PEER_HEADS, PEER_N_KEYS, tm), lambda i, e: (0, 0, i), pipeline_mode=once)
    n_tiles = PEER_N_EXPERTS // PEER_TE
    return pl.pallas_call(
        _peer_mix_kernel,
        grid=(m // tm, n_tiles + 1),
        in_specs=[pl.BlockSpec((d, tm), lambda i, e: (0, i), pipeline_mode=once),
                  pl.BlockSpec((tm, d), lambda i, e: (i, 0), pipeline_mode=once),
                  pl.BlockSpec((1, d), lambda i, e: (0, 0)),
                  pl.BlockSpec((PEER_TE, d), lambda i, e: (jnp.minimum(e, n_tiles - 1), 0)),
                  pl.BlockSpec((PEER_TE, d), lambda i, e: (jnp.maximum(e - 1, 0), 0)),
                  sel, sel, sel, sel],
        out_specs=pl.BlockSpec((tm, d), lambda i, e: (i, 0), pipeline_mode=once),
        out_shape=jax.ShapeDtypeStruct((m, d), F32),
        scratch_shapes=[pltpu.VMEM((tm, PEER_TE), BF16)],
        compiler_params=_params("parallel", "arbitrary"),
        name="peer_mix",
    )(xt, h, g_final.reshape(1, d), u, v, na, e1z, r2, e2)


def _heads(x, n):
    return x.reshape(x.shape[0], x.shape[1], n, HEAD_DIM)


def kernel(x_prompt, x_sample, cache_win_k, cache_win_v, state_hgrn, cache_mem_k, cache_mem_v, mem_prompt, norm_mix, w_in, lb_table, g_att_out, g_rec_out, w_out, norm_mem_x, norm_mem_kv, w_mem_q, w_mem_k, w_mem_v, w_mem_o, norm_ffn, peer_w_query, peer_sub_keys, peer_u, peer_v, norm_final):
    bp, sp, d = x_prompt.shape
    bs, ts, _ = x_sample.shape
    depth = w_in.shape[0]
    assert depth == 1
    rope_p = _rope_tables(jnp.arange(sp, dtype=jnp.int32))
    rope_s = _rope_tables(jnp.tile(PAST_LEN + jnp.arange(ts, dtype=jnp.int32), bs))
    lower_bounds = jnp.cumsum(jax.nn.softmax(lb_table.astype(F32), axis=0), axis=0)
    hp = x_prompt.reshape(bp * sp, d)
    hs = x_sample.reshape(bs * ts, d)
    l = 0
    lb = lower_bounds[l]
    w_in_b, w_out_b = w_in[l].astype(BF16), w_out[l].astype(BF16)

    zp = norm_matmul(hp, norm_mix[l], w_in_b, rope=rope_p, rope_cols=2 * ATT_WIDTH, tn=1024).reshape(bp, sp, IN_COLS)
    a_p = attention_prompt(zp, g_att_out[l])
    r_p, st_p = hgrn(zp, lb, g_rec_out[l], jnp.zeros((bp, N_REC_HEADS, HEAD_DIM, HEAD_DIM), F32),
                     chunk=32, valid=32, l_blk=1024, heads=4, out_dtype=BF16)
    hp = matmul_residual((a_p.reshape(bp * sp, ATT_WIDTH), r_p.reshape(bp * sp, REC_WIDTH)), w_out_b, hp, tn=1024)
    wk_p = _heads(zp[:, sp - WIN_BUF:, ATT_WIDTH:2 * ATT_WIDTH], N_ATT_HEADS)
    wv_p = _heads(zp[:, sp - WIN_BUF:, 2 * ATT_WIDTH:3 * ATT_WIDTH], N_ATT_HEADS)

    zs = norm_matmul(hs, norm_mix[l], w_in_b, rope=rope_s, rope_cols=2 * ATT_WIDTH).reshape(bs, ts, IN_COLS)
    a_s, wk_s, wv_s = attention_sample(_heads(zs[..., :ATT_WIDTH], N_ATT_HEADS),
                                       _heads(zs[..., ATT_WIDTH:2 * ATT_WIDTH], N_ATT_HEADS),
                                       _heads(zs[..., 2 * ATT_WIDTH:3 * ATT_WIDTH], N_ATT_HEADS),
                                       cache_win_k[l], cache_win_v[l], g_att_out[l])
    pad_t = 16
    zs_pad = jnp.pad(zs, ((0, 0), (0, pad_t - ts), (0, 0)))
    r_s, st_s = hgrn(zs_pad, lb, g_rec_out[l], state_hgrn[l], chunk=pad_t, valid=ts, l_blk=pad_t,
                     heads=N_REC_HEADS, out_dtype=F32)
    mix_s = jnp.concatenate([a_s, r_s[:, :ts]], axis=-1).reshape(bs * ts, d).astype(BF16)
    hs = matmul_residual(mix_s, w_out_b, hs)

    w_kv = jnp.concatenate([w_mem_k[l], w_mem_v[l]], axis=1).astype(BF16)
    mkv = norm_matmul(mem_prompt.reshape(bp * N_MEM, d), norm_mem_kv[l], w_kv)
    mk_p = mkv[:, :MEM_WIDTH].reshape(bp, N_MEM, MEM_WIDTH)
    mv_p = mkv[:, MEM_WIDTH:].reshape(bp, N_MEM, MEM_WIDTH)
    w_q_b, w_o_b = w_mem_q[l].astype(BF16), w_mem_o[l].astype(BF16)
    hp = mem_block(hp, norm_mem_x[l], w_q_b, mk_p, mv_p, w_o_b, rows_per_batch=sp)
    qm_s = norm_matmul(hs, norm_mem_x[l], w_q_b).reshape(bs, ts, MEM_WIDTH)
    qm_s = jnp.pad(qm_s, ((0, 0), (0, pad_t - ts), (0, 0)))
    om_s = mem_attention(qm_s, cache_mem_k[l].reshape(bs, N_MEM, MEM_WIDTH), cache_mem_v[l].reshape(bs, N_MEM, MEM_WIDTH), tq=pad_t)
    hs = matmul_residual(om_s[:, :ts].reshape(bs * ts, MEM_WIDTH), w_o_b, hs)

    w_pq = peer_w_query[l].astype(BF16)
    u_b, v_b = peer_u[l].astype(BF16), peer_v[l].astype(BF16)
    qry_p, xt_p = norm_matmul(hp, norm_ffn[l], w_pq, return_normed=True)
    y_p = peer_mix(xt_p, hp, norm_final, u_b, v_b, *peer_topk(qry_p, peer_sub_keys[l]), tm=512)
    qry_s, xt_s = norm_matmul(hs, norm_ffn[l], w_pq, return_normed=True)
    y_s = peer_mix(xt_s, hs, norm_final, u_b, v_b, *peer_topk(qry_s, peer_sub_keys[l]), tm=128)

    return (y_p.reshape(bp, sp, d), y_s.reshape(bs, ts, d),
            wk_p[None], wv_p[None], st_p[None],
            _heads(mk_p, MEM_HEADS)[None], _heads(mv_p, MEM_HEADS)[None],
            wk_s[None], wv_s[None], st_s[None])
```

```python
import functools

import jax
import jax.numpy as jnp
import numpy as np
from jax import lax
from jax.experimental import pallas as pl
from jax.experimental.pallas import tpu as pltpu

F32 = jnp.float32
BF16 = jnp.bfloat16

D_MODEL = 4096
HEAD_DIM = 128
ATT_WIDTH = 2048
REC_WIDTH = 2048
N_ATT_HEADS = 16
N_REC_HEADS = 16
IN_COLS = 3 * ATT_WIDTH + 4 * REC_WIDTH
DILATED_PATTERNS = ((128, 1), (512, 4), (2048, 16))
WIN_BUF = 2048
PAST_LEN = 8192
ATT_SCALE = HEAD_DIM ** -0.5
ROT_HALF = HEAD_DIM // 8
ROPE_THETA = 500000.0
N_MEM = 256
MEM_HEADS = 4
MEM_WIDTH = 512
PEER_N_KEYS = 128
PEER_N_EXPERTS = PEER_N_KEYS * PEER_N_KEYS
PEER_HEADS = 8
PEER_TOPK = 16
EPS = 1e-6
NEG = -1e30

LANES = 128
VMEM_LIMIT = 56 * 1024 * 1024

NT_DIMS = (((1,), (1,)), ((), ()))
TN_DIMS = (((0,), (0,)), ((), ()))


def _params(*sem):
    return pltpu.CompilerParams(dimension_semantics=sem, vmem_limit_bytes=VMEM_LIMIT)


def _rms(x, g):
    return x * lax.rsqrt(jnp.mean(x * x, axis=-1, keepdims=True) + EPS) * g


def _norm_matmul_kernel(*refs, rope_tiles, heads_per_tile, emit_transposed):
    refs = list(refs)
    xn_ref = refs.pop()
    xt_ref = refs.pop() if emit_transposed else None
    if rope_tiles:
        x_ref, g_ref, w_ref, cos_ref, sa_ref, sb_ref, o_ref = refs
    else:
        x_ref, g_ref, w_ref, o_ref = refs
    j = pl.program_id(1)

    @pl.when(j == 0)
    def _():
        xn_ref[...] = _rms(x_ref[...], g_ref[...]).astype(BF16)
        if emit_transposed:
            xt_ref[...] = xn_ref[...].T

    o_ref[...] = jnp.dot(xn_ref[...], w_ref[...], preferred_element_type=F32).astype(o_ref.dtype)
    if rope_tiles:
        @pl.when(j < rope_tiles)
        def _():
            cos, sa, sb = cos_ref[...], sa_ref[...], sb_ref[...]
            for h in range(heads_per_tile):
                zh = o_ref[:, h * LANES:(h + 1) * LANES]
                o_ref[:, h * LANES:(h + 1) * LANES] = (
                    zh * cos + pltpu.roll(zh, LANES - ROT_HALF, 1) * sa + pltpu.roll(zh, ROT_HALF, 1) * sb)


def norm_matmul(x, g, w, *, rope=None, rope_cols=0, out_dtype=F32, tm=512, tn=512, return_normed=False):
    m, k = x.shape
    n = w.shape[1]
    tm, tn = min(tm, m), min(tn, n)
    assert m % tm == 0 and n % tn == 0 and rope_cols % tn == 0
    in_specs = [pl.BlockSpec((tm, k), lambda i, j: (i, 0)),
                pl.BlockSpec((1, k), lambda i, j: (0, 0)),
                pl.BlockSpec((k, tn), lambda i, j: (0, j))]
    args = [x, g.reshape(1, k), w]
    rope_tiles = 0
    if rope is not None:
        cos, sa, sb = rope
        period = cos.shape[0] // tm
        rope_tiles = rope_cols // tn
        spec = pl.BlockSpec((tm, LANES), lambda i, j: (i % period, 0))
        in_specs += [spec, spec, spec]
        args += [cos, sa, sb]
    out_specs = [pl.BlockSpec((tm, tn), lambda i, j: (i, j))]
    out_shape = [jax.ShapeDtypeStruct((m, n), out_dtype)]
    if return_normed:
        out_specs.append(pl.BlockSpec((k, tm), lambda i, j: (0, i)))
        out_shape.append(jax.ShapeDtypeStruct((k, m), BF16))
    out = pl.pallas_call(
        functools.partial(_norm_matmul_kernel, rope_tiles=rope_tiles, heads_per_tile=tn // LANES,
                          emit_transposed=return_normed),
        grid=(m // tm, n // tn),
        in_specs=in_specs,
        out_specs=out_specs,
        out_shape=out_shape,
        scratch_shapes=[pltpu.VMEM((tm, k), BF16)],
        compiler_params=_params("parallel", "arbitrary"),
        name="norm_matmul",
    )(*args)
    return out if return_normed else out[0]


def _rope_tables(pos):
    inv = ROPE_THETA ** (-jnp.arange(ROT_HALF, dtype=F32) / ROT_HALF)
    ang = pos.astype(F32)[:, None] * inv[None, :]
    cos, sin = jnp.cos(ang), jnp.sin(ang)
    t = pos.shape[0]
    rest = LANES - 2 * ROT_HALF
    cos_t = jnp.concatenate([cos, cos, jnp.ones((t, rest), F32)], axis=1)
    sa = jnp.concatenate([-sin, jnp.zeros((t, LANES - ROT_HALF), F32)], axis=1)
    sb = jnp.concatenate([jnp.zeros((t, ROT_HALF), F32), sin, jnp.zeros((t, rest), F32)], axis=1)
    return cos_t, sa, sb


def _matmul_residual_kernel(*refs):
    *a_refs, w_ref, r_ref, o_ref = refs
    acc = r_ref[...]
    row = 0
    for a_ref in a_refs:
        k = a_ref.shape[1]
        acc = acc + jnp.dot(a_ref[...], w_ref[row:row + k, :], preferred_element_type=F32)
        row += k
    o_ref[...] = acc


def matmul_residual(a, w, res, *, tm=512, tn=512):
    pieces = a if isinstance(a, (tuple, list)) else (a,)
    m = pieces[0].shape[0]
    k, n = w.shape
    tm, tn = min(tm, m), min(tn, n)
    assert m % tm == 0 and n % tn == 0 and sum(p.shape[1] for p in pieces) == k
    return pl.pallas_call(
        _matmul_residual_kernel,
        grid=(m // tm, n // tn),
        in_specs=[pl.BlockSpec((tm, p.shape[1]), lambda i, j: (i, 0)) for p in pieces]
        + [pl.BlockSpec((k, tn), lambda i, j: (0, j)),
           pl.BlockSpec((tm, tn), lambda i, j: (i, j))],
        out_specs=pl.BlockSpec((tm, tn), lambda i, j: (i, j)),
        out_shape=jax.ShapeDtypeStruct((m, n), F32),
        compiler_params=_params("parallel", "parallel"),
        name="matmul_residual",
    )(*pieces, w, res)


ATT_TQ = 256
ATT_BACK = WIN_BUF // ATT_TQ
ATT_WIN = WIN_BUF + ATT_TQ


def _distance_multiplicity(delta):
    c = np.zeros(delta.shape, np.float32)
    for window, dil in DILATED_PATTERNS:
        c = c + ((delta >= 0) & (delta <= window) & (delta % dil == 0)).astype(np.float32)
    return c


def _log_multiplicity(delta):
    c = _distance_multiplicity(delta)
    return np.where(c > 0, np.log(np.maximum(c, 1.0)), NEG).astype(np.float32)


def _attn_prompt_kernel(q_ref, k_ref, v_ref, g_ref, bias_ref, o_ref, kb_ref, vb_ref):
    i = pl.program_id(2)

    @pl.when(i == 0)
    def _():
        kb_ref[0:WIN_BUF] = jnp.zeros((WIN_BUF, HEAD_DIM), BF16)
        vb_ref[0:WIN_BUF] = jnp.zeros((WIN_BUF, HEAD_DIM), BF16)
        kb_ref[WIN_BUF:] = k_ref[...].astype(BF16)
        vb_ref[WIN_BUF:] = v_ref[...].astype(BF16)

    q = (q_ref[...] * ATT_SCALE).astype(BF16)
    win = pl.ds(pl.multiple_of(i * ATT_TQ, ATT_TQ), ATT_WIN)
    s = lax.dot_general(q, kb_ref[win, :], NT_DIMS, preferred_element_type=F32) + bias_ref[...]
    col = lax.broadcasted_iota(jnp.int32, (1, ATT_WIN), 1)
    s = jnp.where(col >= WIN_BUF - i * ATT_TQ, s, NEG)
    p = jnp.exp(s - jnp.max(s, axis=-1, keepdims=True))
    l = jnp.sum(p, axis=-1, keepdims=True)
    o = jnp.dot(p.astype(BF16), vb_ref[win, :], preferred_element_type=F32) / l
    o_ref[...] = _rms(o, g_ref[...]).astype(o_ref.dtype)


def attention_prompt(z, g_att):
    b, s, _ = z.shape
    rq = np.arange(ATT_TQ)[:, None]
    col = np.arange(ATT_WIN)[None, :]
    bias = _log_multiplicity(WIN_BUF + rq - col)
    kcol, vcol = ATT_WIDTH // HEAD_DIM, 2 * ATT_WIDTH // HEAD_DIM
    return pl.pallas_call(
        _attn_prompt_kernel,
        grid=(b, N_ATT_HEADS, s // ATT_TQ),
        in_specs=[pl.BlockSpec((None, ATT_TQ, HEAD_DIM), lambda bi, h, i: (bi, i, h)),
                  pl.BlockSpec((None, s, HEAD_DIM), lambda bi, h, i: (bi, 0, kcol + h)),
                  pl.BlockSpec((None, s, HEAD_DIM), lambda bi, h, i: (bi, 0, vcol + h)),
                  pl.BlockSpec((1, HEAD_DIM), lambda bi, h, i: (0, h)),
                  pl.BlockSpec((ATT_TQ, ATT_WIN), lambda bi, h, i: (0, 0))],
        out_specs=pl.BlockSpec((None, ATT_TQ, HEAD_DIM), lambda bi, h, i: (bi, i, h)),
        out_shape=jax.ShapeDtypeStruct((b, s, ATT_WIDTH), BF16),
        scratch_shapes=[pltpu.VMEM((WIN_BUF + s, HEAD_DIM), BF16), pltpu.VMEM((WIN_BUF + s, HEAD_DIM), BF16)],
        compiler_params=_params("parallel", "parallel", "arbitrary"),
        name="attention_prompt",
    )(z, z, z, g_att.reshape(1, ATT_WIDTH), bias)


SMP_TB = 512
SMP_ROWS = SMP_TB * N_ATT_HEADS


def _attn_sample_kernel(q_ref, kc_ref, vc_ref, kx_ref, vx_ref, kn_ref, vn_ref, bias_ref, biasn_ref, g_ref,
                        wk_ref, wv_ref, o_ref, m_ref, l_ref, acc_ref, *, t_new):
    kblk = pl.program_id(1)
    last = pl.num_programs(1) - 1

    @pl.when(kblk == 0)
    def _():
        m_ref[...] = jnp.full(m_ref.shape, NEG, F32)
        l_ref[...] = jnp.zeros(l_ref.shape, F32)
        acc_ref[...] = jnp.zeros(acc_ref.shape, F32)

    wk_ref[0:SMP_TB - t_new] = kc_ref[t_new:SMP_TB]
    wv_ref[0:SMP_TB - t_new] = vc_ref[t_new:SMP_TB]

    @pl.when(kblk < last)
    def _():
        wk_ref[SMP_TB - t_new:SMP_TB] = kx_ref[...]
        wv_ref[SMP_TB - t_new:SMP_TB] = vx_ref[...]

    @pl.when(kblk == last)
    def _():
        wk_ref[SMP_TB - t_new:SMP_TB] = kn_ref[...]
        wv_ref[SMP_TB - t_new:SMP_TB] = vn_ref[...]

    q = q_ref[...]

    def update(k2, v2, bias):
        s = lax.dot_general(q, k2.astype(BF16), NT_DIMS, preferred_element_type=F32) + bias
        m_old = m_ref[...]
        m_new = jnp.maximum(m_old, jnp.max(s, axis=-1, keepdims=True))
        alpha = jnp.exp(m_old - m_new)
        p = jnp.exp(s - m_new)
        l_ref[...] = alpha * l_ref[...] + jnp.sum(p, axis=-1, keepdims=True)
        acc_ref[...] = alpha * acc_ref[...] + jnp.dot(p.astype(BF16), v2.astype(BF16), preferred_element_type=F32)
        m_ref[...] = m_new

    update(kc_ref[...].reshape(SMP_ROWS, HEAD_DIM), vc_ref[...].reshape(SMP_ROWS, HEAD_DIM), bias_ref[...])

    @pl.when(kblk == last)
    def _():
        update(kn_ref[...].reshape(t_new * N_ATT_HEADS, HEAD_DIM), vn_ref[...].reshape(t_new * N_ATT_HEADS, HEAD_DIM),
               biasn_ref[...])
        o_ref[...] = _rms(acc_ref[...] / l_ref[...], g_ref[...])


def attention_sample(q, k_new, v_new, cache_k, cache_v, g_att):
    bd, t_new, nh, dh = q.shape
    nblk = WIN_BUF // SMP_TB
    qt = (q * ATT_SCALE).transpose(0, 2, 1, 3).reshape(bd, nh * t_new, dh).astype(BF16)
    hq = np.arange(nh * t_new)[:, None] // t_new
    jq = np.arange(nh * t_new)[:, None] % t_new
    def make_bias(tok, hk):
        return np.where(hq == hk, _log_multiplicity(WIN_BUF + jq - tok), np.float32(NEG))
    cols = np.arange(WIN_BUF * nh)[None, :]
    bias = make_bias(cols // nh, cols % nh).reshape(nh * t_new, nblk, SMP_ROWS).transpose(1, 0, 2)
    cols_n = np.arange(t_new * nh)[None, :]
    bias_n = make_bias(WIN_BUF + cols_n // nh, cols_n % nh)
    g_rows = jnp.repeat(g_att.reshape(nh, dh), t_new, axis=0)
    sub = SMP_TB // t_new
    cache_blk = pl.BlockSpec((None, SMP_TB, nh, dh), lambda b, k: (b, k, 0, 0))
    next_blk = pl.BlockSpec((None, t_new, nh, dh), lambda b, k: (b, jnp.minimum((k + 1) * sub, WIN_BUF // t_new - 1), 0, 0))
    new_blk = pl.BlockSpec((None, t_new, nh, dh), lambda b, k: (b, 0, 0, 0))
    wk, wv, o = pl.pallas_call(
        functools.partial(_attn_sample_kernel, t_new=t_new),
        grid=(bd, nblk),
        in_specs=[pl.BlockSpec((None, nh * t_new, dh), lambda b, k: (b, 0, 0)),
                  cache_blk, cache_blk, next_blk, next_blk, new_blk, new_blk,
                  pl.BlockSpec((None, nh * t_new, SMP_ROWS), lambda b, k: (k, 0, 0)),
                  pl.BlockSpec((nh * t_new, nh * t_new), lambda b, k: (0, 0)),
                  pl.BlockSpec((nh * t_new, dh), lambda b, k: (0, 0))],
        out_specs=[cache_blk, cache_blk, pl.BlockSpec((None, nh * t_new, dh), lambda b, k: (b, 0, 0))],
        out_shape=[jax.ShapeDtypeStruct(cache_k.shape, F32), jax.ShapeDtypeStruct(cache_v.shape, F32),
                   jax.ShapeDtypeStruct((bd, nh * t_new, dh), F32)],
        scratch_shapes=[pltpu.VMEM((nh * t_new, 1), F32), pltpu.VMEM((nh * t_new, 1), F32),
                        pltpu.VMEM((nh * t_new, dh), F32)],
        compiler_params=_params("parallel", "arbitrary"),
        name="attention_sample",
    )(qt, cache_k, cache_v, cache_k, cache_v, k_new, v_new, bias, bias_n, g_rows)
    a = o.reshape(bd, nh, t_new, dh).transpose(0, 2, 1, 3).reshape(bd, t_new, nh * dh)
    return a, wk, wv


def _sigmoid_pair(x):
    e = jnp.exp(-jnp.abs(x))
    r = 1.0 / (1.0 + e)
    er = e * r
    pos = x >= 0
    return jnp.where(pos, r, er), jnp.where(pos, er, r)


REC_GROUP = 128
SUBLANES = 8


def _hgrn_kernel(q_ref, f_ref, i_ref, gz_ref, lb_ref, g_ref, s0_ref, r_ref, sfin_ref,
                 st_ref, oin_ref, qd_ref, u_ref, dec_ref, *, chunk, valid):
    lblk = pl.program_id(2)
    l_blk = q_ref.shape[0]
    grp = min(REC_GROUP, l_blk)
    per = grp // chunk

    @pl.when(lblk == 0)
    def _():
        for h in range(st_ref.shape[0]):
            st_ref[h] = s0_ref[h].T

    row = lax.broadcasted_iota(jnp.int32, (grp, grp), 0)
    col = lax.broadcasted_iota(jnp.int32, (grp, grp), 1)
    same_chunk_causal = (row // chunk == col // chunk) & (row >= col)
    pos = lax.broadcasted_iota(jnp.int32, (grp, HEAD_DIM), 0) % chunk

    def group_body(g, carry):
        rows = pl.ds(pl.multiple_of(g * grp, grp), grp)
        for h in range(st_ref.shape[0]):
            cols = slice(h * HEAD_DIM, (h + 1) * HEAD_DIM)
            lb = lb_ref[:, cols]
            sig, sig_neg = _sigmoid_pair(f_ref[rows, cols])
            logf = jnp.log(lb + (1.0 - lb) * sig)
            kr = (1.0 - lb) * sig_neg
            if valid < chunk:
                logf = jnp.where(pos < valid, logf, 0.0)
                kr = jnp.where(pos < valid, kr, 0.0)
            bcum = logf
            d = 1
            while d < chunk:
                bcum = bcum + jnp.where(pos >= d, pltpu.roll(bcum, d, 0), 0.0)
                d *= 2
            last = bcum.reshape(per, chunk, HEAD_DIM)[:, chunk - 1:chunk, :]
            blast = jnp.broadcast_to(last, (per, chunk, HEAD_DIM)).reshape(grp, HEAD_DIM)
            qz = q_ref[rows, cols]
            q = qz * _sigmoid_pair(qz)[0]
            v = i_ref[rows, cols].astype(BF16)
            q_dec = (q * jnp.exp(bcum)).astype(BF16)
            k_inv = (kr * jnp.exp(-bcum)).astype(BF16)
            k_dec = (kr * jnp.exp(blast - bcum)).astype(BF16)
            att = lax.dot_general(q_dec, k_inv, NT_DIMS, preferred_element_type=F32)
            att = jnp.where(same_chunk_causal, att, 0.0).astype(BF16)
            oin_ref[h, rows, :] = jnp.dot(att, v, preferred_element_type=F32)
            qd_ref[h, rows, :] = q_dec
            for c in range(per):
                cr = slice(c * chunk, (c + 1) * chunk)
                u_ref[h, g * per + c] = lax.dot_general(v[cr], k_dec[cr], TN_DIMS, preferred_element_type=F32)
                dec_ref[h, g * per + c] = jnp.broadcast_to(jnp.exp(last[c]), (SUBLANES, HEAD_DIM))
        return carry

    lax.fori_loop(0, l_blk // grp, group_body, 0)

    def chunk_body(c, carry):
        rows = pl.ds(pl.multiple_of(c * chunk, chunk), chunk)
        for h in range(st_ref.shape[0]):
            cols = slice(h * HEAD_DIM, (h + 1) * HEAD_DIM)
            st = st_ref[h]
            o = oin_ref[h, rows, :] + lax.dot_general(qd_ref[h, rows, :], st.astype(BF16), NT_DIMS,
                                                      preferred_element_type=F32)
            st3 = st.reshape(HEAD_DIM // SUBLANES, SUBLANES, HEAD_DIM) * dec_ref[h, c][None]
            st_ref[h] = st3.reshape(HEAD_DIM, HEAD_DIM) + u_ref[h, c]
            gz = gz_ref[rows, cols]
            gate = gz * _sigmoid_pair(gz)[0]
            r_ref[rows, cols] = (_rms(o, g_ref[:, cols]) * gate).astype(r_ref.dtype)
        return carry

    lax.fori_loop(0, l_blk // chunk, chunk_body, 0, unroll=min(8, l_blk // chunk))

    @pl.when(lblk == pl.num_programs(2) - 1)
    def _():
        for h in range(st_ref.shape[0]):
            sfin_ref[h] = st_ref[h].T


def hgrn(z, lb, g_rec, s0, *, chunk, valid, l_blk, heads, out_dtype):
    b, l, _ = z.shape
    assert l % l_blk == 0 and l_blk % chunk == 0 and N_REC_HEADS % heads == 0
    width = heads * HEAD_DIM
    base = 3 * ATT_WIDTH // width
    per = REC_WIDTH // width
    zspec = lambda k: pl.BlockSpec((None, l_blk, width), lambda bi, hg, li: (bi, li, base + k * per + hg))
    vec = pl.BlockSpec((1, width), lambda bi, hg, li: (0, hg))
    st = pl.BlockSpec((None, heads, HEAD_DIM, HEAD_DIM), lambda bi, hg, li: (bi, hg, 0, 0))
    return pl.pallas_call(
        functools.partial(_hgrn_kernel, chunk=chunk, valid=valid),
        grid=(b, N_REC_HEADS // heads, l // l_blk),
        in_specs=[zspec(0), zspec(1), zspec(2), zspec(3), vec, vec, st],
        out_specs=[pl.BlockSpec((None, l_blk, width), lambda bi, hg, li: (bi, li, hg)), st],
        out_shape=[jax.ShapeDtypeStruct((b, l, REC_WIDTH), out_dtype),
                   jax.ShapeDtypeStruct((b, N_REC_HEADS, HEAD_DIM, HEAD_DIM), F32)],
        scratch_shapes=[pltpu.VMEM((heads, HEAD_DIM, HEAD_DIM), F32),
                        pltpu.VMEM((heads, l_blk, HEAD_DIM), F32),
                        pltpu.VMEM((heads, l_blk, HEAD_DIM), BF16),
                        pltpu.VMEM((heads, l_blk // chunk, HEAD_DIM, HEAD_DIM), F32),
                        pltpu.VMEM((heads, l_blk // chunk, SUBLANES, HEAD_DIM), F32)],
        compiler_params=_params("parallel", "parallel", "arbitrary"),
        name="hgrn",
    )(z, z, z, z, lb.reshape(1, REC_WIDTH), g_rec.reshape(1, REC_WIDTH), s0)


def _mem_attn_kernel(q_ref, k_ref, v_ref, o_ref):
    scale = HEAD_DIM ** -0.5
    for h in range(MEM_HEADS):
        cols = slice(h * HEAD_DIM, (h + 1) * HEAD_DIM)
        q = (q_ref[:, cols] * scale).astype(BF16)
        s = lax.dot_general(q, k_ref[:, cols].astype(BF16), NT_DIMS, preferred_element_type=F32)
        p = jnp.exp(s - jnp.max(s, axis=-1, keepdims=True))
        l = jnp.sum(p, axis=-1, keepdims=True)
        o = jnp.dot(p.astype(BF16), v_ref[:, cols].astype(BF16), preferred_element_type=F32)
        o_ref[:, cols] = (o / l).astype(o_ref.dtype)


def _mem_heads(q, k_ref, v_ref):
    scale = HEAD_DIM ** -0.5
    outs = []
    for h in range(MEM_HEADS):
        cols = slice(h * HEAD_DIM, (h + 1) * HEAD_DIM)
        s = lax.dot_general((q[:, cols] * scale).astype(BF16), k_ref[:, cols].astype(BF16), NT_DIMS,
                            preferred_element_type=F32)
        p = jnp.exp(s - jnp.max(s, axis=-1, keepdims=True))
        l = jnp.sum(p, axis=-1, keepdims=True)
        o = jnp.dot(p.astype(BF16), v_ref[:, cols].astype(BF16), preferred_element_type=F32)
        outs.append((o / l).astype(BF16))
    return jnp.concatenate(outs, axis=1)


def _mem_block_kernel(h_ref, g_ref, wq_ref, k_ref, v_ref, wo_ref, o_ref):
    h = h_ref[...]
    q = jnp.dot(_rms(h, g_ref[...]).astype(BF16), wq_ref[...], preferred_element_type=F32)
    o_ref[...] = h + jnp.dot(_mem_heads(q, k_ref, v_ref), wo_ref[...], preferred_element_type=F32)


def mem_block(h, g, w_q, mk, mv, w_o, *, rows_per_batch, tm=512):
    m, d = h.shape
    w = w_q.shape[1]
    assert rows_per_batch % tm == 0
    per = rows_per_batch // tm
    once = pl.Buffered(1)
    return pl.pallas_call(
        _mem_block_kernel,
        grid=(m // tm,),
        in_specs=[pl.BlockSpec((tm, d), lambda i: (i, 0)),
                  pl.BlockSpec((1, d), lambda i: (0, 0)),
                  pl.BlockSpec((d, w), lambda i: (0, 0), pipeline_mode=once),
                  pl.BlockSpec((None, N_MEM, w), lambda i: (i // per, 0, 0)),
                  pl.BlockSpec((None, N_MEM, w), lambda i: (i // per, 0, 0)),
                  pl.BlockSpec((w, d), lambda i: (0, 0), pipeline_mode=once)],
        out_specs=pl.BlockSpec((tm, d), lambda i: (i, 0)),
        out_shape=jax.ShapeDtypeStruct((m, d), F32),
        compiler_params=_params("parallel"),
        name="mem_block",
    )(h, g.reshape(1, d), w_q, mk, mv, w_o)


def mem_attention(q, mk, mv, *, tq):
    b, t, w = q.shape
    tq = min(tq, t)
    return pl.pallas_call(
        _mem_attn_kernel,
        grid=(b, t // tq),
        in_specs=[pl.BlockSpec((None, tq, w), lambda bi, i: (bi, i, 0)),
                  pl.BlockSpec((None, N_MEM, w), lambda bi, i: (bi, 0, 0)),
                  pl.BlockSpec((None, N_MEM, w), lambda bi, i: (bi, 0, 0))],
        out_specs=pl.BlockSpec((None, tq, w), lambda bi, i: (bi, i, 0)),
        out_shape=jax.ShapeDtypeStruct((b, t, w), BF16),
        compiler_params=_params("parallel", "parallel"),
        name="mem_attention",
    )(q, mk, mv)


PEER_TT = 512
PEER_MARK = 1e30
PEER_WIDE = 8


def _first_max(x, order, sentinel):
    m = jnp.max(x, axis=0, keepdims=True)
    return m, jnp.min(jnp.where(x == m, order, sentinel), axis=0, keepdims=True)


def _top16_cols(scores):
    n, t = scores[0].shape
    row = lax.broadcasted_iota(jnp.int32, (n, t), 0)
    row_out = lax.broadcasted_iota(jnp.int32, (PEER_TOPK, t), 0)

    def body(it, carry):
        mark = (lax.convert_element_type(it, F32) + 1.0) * -PEER_MARK
        out = []
        for s, vals in carry:
            m, first = _first_max(s, row, n)
            out.append((jnp.where(row == first, mark, s), jnp.where(row_out == it, m, vals)))
        return tuple(out)

    done = lax.fori_loop(0, PEER_TOPK, body, tuple((s, jnp.zeros((PEER_TOPK, t), F32)) for s in scores))
    return [(vals, jnp.where(s <= -0.5 * PEER_MARK, jnp.floor(s * (-1.0 / PEER_MARK) - 0.5), float(PEER_TOPK)))
            for s, vals in done]


def _candidates(v1, v2):
    parts = [v1[0:1, :] + v2]
    parts += [v1[r:r + 1, :] + v2[0:PEER_WIDE, :] for r in range(1, PEER_WIDE)]
    parts.append(v1[PEER_WIDE:, :] + v2[0:1, :])
    return jnp.concatenate(parts, axis=0)


def _peer_topk_kernel(q_ref, keys_ref, na_ref, e1z_ref, r2_ref, e2_ref):
    s1_all = lax.dot_general(keys_ref[0].astype(BF16), q_ref[:, 0:LANES].astype(BF16), NT_DIMS,
                             preferred_element_type=F32)
    s2_all = lax.dot_general(keys_ref[1].astype(BF16), q_ref[:, LANES:2 * LANES].astype(BF16), NT_DIMS,
                             preferred_element_type=F32)
    n_chains = q_ref.shape[0] // LANES
    stage1 = []
    for c in range(n_chains):
        lanes = slice(c * LANES, (c + 1) * LANES)
        stage1.append(_top16_cols((s1_all[:, lanes], s2_all[:, lanes])))

    head = PEER_TOPK + (PEER_WIDE - 1) * PEER_WIDE
    n_cand = head + PEER_TOPK - PEER_WIDE
    j = lax.broadcasted_iota(jnp.int32, (n_cand, LANES), 0)
    k = j - PEER_TOPK
    order = jnp.where(j < PEER_TOPK, j,
                      jnp.where(j < head, (1 + k // PEER_WIDE) * PEER_TOPK + k % PEER_WIDE,
                                (j - head + PEER_WIDE) * PEER_TOPK))

    def final_body(it, carry):
        out = []
        for cand, top0, zsum in carry:
            m, first = _first_max(cand, order, PEER_TOPK * PEER_TOPK)
            top0 = jnp.where(it == 0, m, top0)
            out.append((jnp.where(order == first, -jnp.inf, cand), top0, zsum + jnp.exp(m - top0)))
        return tuple(out)

    zeros = jnp.zeros((1, LANES), F32)
    final = lax.fori_loop(0, PEER_TOPK, final_body,
                          tuple((_candidates(v1, v2), zeros, zeros) for (v1, _), (v2, _) in stage1))

    for c in range(n_chains):
        lanes = slice(c * LANES, (c + 1) * LANES)
        (v1, r1), (v2, r2) = stage1[c]
        cand, _, zsum = final[c]
        sel = jnp.where(cand == -jnp.inf, 1.0, 0.0)
        na = jnp.zeros(r1.shape, F32)
        for r in range(PEER_TOPK):
            if r == 0:
                n_r = jnp.sum(sel[0:PEER_TOPK, :], axis=0, keepdims=True)
            elif r < PEER_WIDE:
                lo = PEER_TOPK + (r - 1) * PEER_WIDE
                n_r = jnp.sum(sel[lo:lo + PEER_WIDE, :], axis=0, keepdims=True)
            else:
                n_r = sel[head + r - PEER_WIDE:head + r - PEER_WIDE + 1, :]
            na = jnp.where(r1 == float(r), n_r, na)
        na_ref[:, lanes] = na
        r2_ref[:, lanes] = r2.astype(r2_ref.dtype)
        e1z_ref[:, lanes] = jnp.exp(s1_all[:, lanes] - v1[0:1, :]) / zsum
        e2_ref[:, lanes] = jnp.exp(s2_all[:, lanes] - v2[0:1, :]).astype(e2_ref.dtype)


def peer_topk(qry, sub_keys):
    m = qry.shape[0]
    tt = min(PEER_TT, m)
    out_spec = pl.BlockSpec((None, PEER_N_KEYS, tt), lambda i, h: (h, 0, i))
    shape = (PEER_HEADS, PEER_N_KEYS, m)
    return pl.pallas_call(
        _peer_topk_kernel,
        grid=(m // tt, PEER_HEADS),
        in_specs=[pl.BlockSpec((tt, 2 * PEER_N_KEYS), lambda i, h: (i, h)),
                  pl.BlockSpec((None, 2, PEER_N_KEYS, PEER_N_KEYS), lambda i, h: (h, 0, 0, 0))],
        out_specs=[out_spec, out_spec, out_spec, out_spec],
        out_shape=[jax.ShapeDtypeStruct(shape, F32), jax.ShapeDtypeStruct(shape, F32),
                   jax.ShapeDtypeStruct(shape, BF16), jax.ShapeDtypeStruct(shape, BF16)],
        compiler_params=_params("parallel", "parallel"),
        name="peer_topk",
    )(qry, sub_keys)


PEER_TE = 512
PEER_SUB = PEER_TE // PEER_N_KEYS


PEER_ROWS = 16

_ERF_T = 0.3275911
_ERF_C = (0.254829592, -0.284496736, 1.421413741, -1.453152027, 1.061405429)


def _gelu(x):
    z = x * (2.0 ** -0.5)
    t = 1.0 / (1.0 + _ERF_T * jnp.abs(z))
    half = 0.5 * _ERF_C[4]
    for c in _ERF_C[3::-1]:
        half = half * t + 0.5 * c
    half = half * t * jnp.exp(-(z * z))
    return x * jnp.where(z >= 0, 1.0 - half, half)


def _peer_mix_kernel(xt_ref, h_ref, gf_ref, u_ref, v_ref, na_ref, e1z_ref, r2_ref, e2_ref, o_ref, w_ref):
    e = pl.program_id(1)
    n_tiles = pl.num_programs(1) - 1
    tm, d = o_ref.shape
    groups = PEER_N_KEYS // PEER_ROWS

    @pl.when(e == 0)
    def _():
        o_ref[...] = jnp.zeros(o_ref.shape, F32)
        w_ref[...] = jnp.zeros(w_ref.shape, BF16)

    o_ref[...] += jnp.dot(w_ref[...], v_ref[...], preferred_element_type=F32)

    tile = jnp.minimum(e, n_tiles - 1)
    hid = jnp.dot(u_ref[...], xt_ref[...], preferred_element_type=F32)
    parts = []
    for sub in range(PEER_SUB):
        a = tile * PEER_SUB + sub
        act = _gelu(hid[sub * PEER_N_KEYS:(sub + 1) * PEER_N_KEYS, :]).astype(BF16)
        w = jnp.zeros((groups, PEER_ROWS, tm), BF16)
        for hd in range(PEER_HEADS):
            na = jnp.broadcast_to(na_ref[hd, pl.ds(a, 1), :], (PEER_ROWS, tm)).astype(BF16)
            e1z = jnp.broadcast_to(e1z_ref[hd, pl.ds(a, 1), :], (PEER_ROWS, tm)).astype(BF16)
            r2 = r2_ref[hd].reshape(groups, PEER_ROWS, tm)
            e2 = e2_ref[hd].reshape(groups, PEER_ROWS, tm)
            w = w + jnp.where(r2 < na[None], e2, jnp.zeros_like(e2)) * e1z[None]
        parts.append(w.reshape(PEER_N_KEYS, tm) * act)
    w_ref[...] = jnp.concatenate(parts, axis=0).T

    @pl.when(e == n_tiles)
    def _():
        o_ref[...] = _rms(h_ref[...] + o_ref[...], gf_ref[...])


def peer_mix(xt, h, g_final, u, v, na, e1z, r2, e2, *, tm):
    m, d = h.shape
    tm = min(tm, m)
    once = pl.Buffered(1)
    sel = pl.BlockSpec((PEER_HEADS, PEER_N_KEYS, tm), lambda i, e: (0, 0, i), pipeline_mode=once)
    n_tiles = PEER_N_EXPERTS // PEER_TE
    return pl.pallas_call(
        _peer_mix_kernel,
        grid=(m // tm, n_tiles + 1),
        in_specs=[pl.BlockSpec((d, tm), lambda i, e: (0, i), pipeline_mode=once),
                  pl.BlockSpec((tm, d), lambda i, e: (i, 0), pipeline_mode=once),
                  pl.BlockSpec((1, d), lambda i, e: (0, 0)),
                  pl.BlockSpec((PEER_TE, d), lambda i, e: (jnp.minimum(e, n_tiles - 1), 0)),
                  pl.BlockSpec((PEER_TE, d), lambda i, e: (jnp.maximum(e - 1, 0), 0)),
                  sel, sel, sel, sel],
        out_specs=pl.BlockSpec((tm, d), lambda i, e: (i, 0), pipeline_mode=once),
        out_shape=jax.ShapeDtypeStruct((m, d), F32),
        scratch_shapes=[pltpu.VMEM((tm, PEER_TE), BF16)],
        compiler_params=_params("parallel", "arbitrary"),
        name="peer_mix",
    )(xt, h, g_final.reshape(1, d), u, v, na, e1z, r2, e2)


def _heads(x, n):
    return x.reshape(x.shape[0], x.shape[1], n, HEAD_DIM)


def kernel(x_prompt, x_sample, cache_win_k, cache_win_v, state_hgrn, cache_mem_k, cache_mem_v, mem_prompt, norm_mix, w_in, lb_table, g_att_out, g_rec_out, w_out, norm_mem_x, norm_mem_kv, w_mem_q, w_mem_k, w_mem_v, w_mem_o, norm_ffn, peer_w_query, peer_sub_keys, peer_u, peer_v, norm_final):
    bp, sp, d = x_prompt.shape
    bs, ts, _ = x_sample.shape
    depth = w_in.shape[0]
    assert depth == 1
    rope_p = _rope_tables(jnp.arange(sp, dtype=jnp.int32))
    rope_s = _rope_tables(jnp.tile(PAST_LEN + jnp.arange(ts, dtype=jnp.int32), bs))
    lower_bounds = jnp.cumsum(jax.nn.softmax(lb_table.astype(F32), axis=0), axis=0)
    hp = x_prompt.reshape(bp * sp, d)
    hs = x_sample.reshape(bs * ts, d)
    l = 0
    lb = lower_bounds[l]
    w_in_b, w_out_b = w_in[l].astype(BF16), w_out[l].astype(BF16)

    zp = norm_matmul(hp, norm_mix[l], w_in_b, rope=rope_p, rope_cols=2 * ATT_WIDTH, tn=1024).reshape(bp, sp, IN_COLS)
    a_p = attention_prompt(zp, g_att_out[l])
    r_p, st_p = hgrn(zp, lb, g_rec_out[l], jnp.zeros((bp, N_REC_HEADS, HEAD_DIM, HEAD_DIM), F32),
                     chunk=32, valid=32, l_blk=1024, heads=4, out_dtype=BF16)
    hp = matmul_residual((a_p.reshape(bp * sp, ATT_WIDTH), r_p.reshape(bp * sp, REC_WIDTH)), w_out_b, hp, tn=1024)
    wk_p = _heads(zp[:, sp - WIN_BUF:, ATT_WIDTH:2 * ATT_WIDTH], N_ATT_HEADS)
    wv_p = _heads(zp[:, sp - WIN_BUF:, 2 * ATT_WIDTH:3 * ATT_WIDTH], N_ATT_HEADS)

    zs = norm_matmul(hs, norm_mix[l], w_in_b, rope=rope_s, rope_cols=2 * ATT_WIDTH).reshape(bs, ts, IN_COLS)
    a_s, wk_s, wv_s = attention_sample(_heads(zs[..., :ATT_WIDTH], N_ATT_HEADS),
                                       _heads(zs[..., ATT_WIDTH:2 * ATT_WIDTH], N_ATT_HEADS),
                                       _heads(zs[..., 2 * ATT_WIDTH:3 * ATT_WIDTH], N_ATT_HEADS),
                                       cache_win_k[l], cache_win_v[l], g_att_out[l])
    pad_t = 16
    zs_pad = jnp.pad(zs, ((0, 0), (0, pad_t - ts), (0, 0)))
    r_s, st_s = hgrn(zs_pad, lb, g_rec_out[l], state_hgrn[l], chunk=pad_t, valid=ts, l_blk=pad_t,
                     heads=N_REC_HEADS, out_dtype=F32)
    mix_s = jnp.concatenate([a_s, r_s[:, :ts]], axis=-1).reshape(bs * ts, d).astype(BF16)
    hs = matmul_residual(mix_s, w_out_b, hs)

    w_kv = jnp.concatenate([w_mem_k[l], w_mem_v[l]], axis=1).astype(BF16)
    mkv = norm_matmul(mem_prompt.reshape(bp * N_MEM, d), norm_mem_kv[l], w_kv)
    mk_p = mkv[:, :MEM_WIDTH].reshape(bp, N_MEM, MEM_WIDTH)
    mv_p = mkv[:, MEM_WIDTH:].reshape(bp, N_MEM, MEM_WIDTH)
    w_q_b, w_o_b = w_mem_q[l].astype(BF16), w_mem_o[l].astype(BF16)
    hp = mem_block(hp, norm_mem_x[l], w_q_b, mk_p, mv_p, w_o_b, rows_per_batch=sp)
    qm_s = norm_matmul(hs, norm_mem_x[l], w_q_b).reshape(bs, ts, MEM_WIDTH)
    qm_s = jnp.pad(qm_s, ((0, 0), (0, pad_t - ts), (0, 0)))
    om_s = mem_attention(qm_s, cache_mem_k[l].reshape(bs, N_MEM, MEM_WIDTH), cache_mem_v[l].reshape(bs, N_MEM, MEM_WIDTH), tq=pad_t)
    hs = matmul_residual(om_s[:, :ts].reshape(bs * ts, MEM_WIDTH), w_o_b, hs)

    w_pq = peer_w_query[l].astype(BF16)
    u_b, v_b = peer_u[l].astype(BF16), peer_v[l].astype(BF16)
    qry_p, xt_p = norm_matmul(hp, norm_ffn[l], w_pq, return_normed=True)
    y_p = peer_mix(xt_p, hp, norm_final, u_b, v_b, *peer_topk(qry_p, peer_sub_keys[l]), tm=512)
    qry_s, xt_s = norm_matmul(hs, norm_ffn[l], w_pq, return_normed=True)
    y_s = peer_mix(xt_s, hs, norm_final, u_b, v_b, *peer_topk(qry_s, peer_sub_keys[l]), tm=128)

    return (y_p.reshape(bp, sp, d), y_s.reshape(bs, ts, d),
            wk_p[None], wv_p[None], st_p[None],
            _heads(mk_p, MEM_HEADS)[None], _heads(mv_p, MEM_HEADS)[None],
            wk_s[None], wv_s[None], st_s[None])
```
